```python
import jax, jax.numpy as jnp
from jax import lax
import numpy as np

D_MODEL = 1024
BATCH = 32
SEQ = 2048
DEPTH = 2

CTX_LEN = 256
GRID_W = 64
EPS = 1e-6
D_HGRN = 512
HGRN_HEADS = 4
HGRN_DK = D_HGRN // HGRN_HEADS
HGRN_DV = D_HGRN // HGRN_HEADS
HGRN_SCALE = HGRN_DK ** -0.5
CHUNK = 64
D_POOL = 512
POOL_WINDOWS = (2, 4, 8, 16)
POOL_GROUP = D_POOL // len(POOL_WINDOWS)
IN_COLS = 5 * D_HGRN + D_POOL + 2 * D_MODEL
SPLIT_IDX = (D_HGRN, 2 * D_HGRN, 3 * D_HGRN, 4 * D_HGRN, 5 * D_HGRN, 5 * D_HGRN + D_POOL, 5 * D_HGRN + D_POOL + D_MODEL)
D_FF = 2816
N_EXPERTS = 8
TOP_K = 2
D_EXPERT = 3584
N_DENSE = (DEPTH + 1) // 2
N_MOE = DEPTH // 2

kernel_name = "hybrid_hgrn2_pool_moe_diffusion_trunk"


def rms_norm(x, gain):
    xf = x.astype(jnp.float32)
    y = xf * lax.rsqrt(jnp.mean(xf * xf, axis=-1, keepdims=True) + EPS)
    return (y * gain.astype(jnp.float32)).astype(x.dtype)


def head_rms_norm(o, gain):
    b, n, _ = o.shape
    oh = o.reshape(b, n, HGRN_HEADS, HGRN_DV).astype(jnp.float32)
    oh = oh * lax.rsqrt(jnp.mean(oh * oh, axis=-1, keepdims=True) + EPS)
    return oh.reshape(b, n, D_HGRN) * gain.astype(jnp.float32)


def to_heads(a):
    b, n, _ = a.shape
    return a.reshape(b, n, HGRN_HEADS, -1).transpose(0, 2, 1, 3)


def from_heads(a):
    b, h, n, d = a.shape
    return a.transpose(0, 2, 1, 3).reshape(b, n, h * d)


def flip_seq(a):
    return jnp.flip(a, axis=2)


def forget_gate(z, lb):
    zf = z.astype(jnp.float32)
    lbf = lb.astype(jnp.float32)
    log_f = jnp.logaddexp(jnp.log(lbf), jnp.log1p(-lbf) + jax.nn.log_sigmoid(zf))
    k = (1.0 - lbf) * jax.nn.sigmoid(-zf)
    return log_f, k


def recurrence_inputs(f_fwd, f_bwd, i, lb):
    lf_f, k_f = forget_gate(f_fwd, lb[0])
    lf_b, k_b = forget_gate(f_bwd, lb[1])
    return to_heads(k_f), to_heads(lf_f), to_heads(k_b), to_heads(lf_b), to_heads(i)


def gla_scan(q, k, v, log_f, s0):
    b, h, n, dk = q.shape
    nc = n // CHUNK

    def to_chunks(a):
        return jnp.moveaxis(a.reshape(b, h, nc, CHUNK, a.shape[-1]), 2, 0)

    causal = jnp.tril(jnp.ones((CHUNK, CHUNK), dtype=bool))[:, :, None]

    def step(s, inp):
        qc, kc, vc, gc = inp
        cum = jnp.cumsum(gc.astype(jnp.float32), axis=2)
        o_inter = jnp.einsum('bhtk,bhkv->bhtv', qc * jnp.exp(cum), s)
        diff = cum[:, :, :, None, :] - cum[:, :, None, :, :]
        decay = jnp.exp(jnp.where(causal, diff, -jnp.inf))
        att = jnp.einsum('bhtk,bhsk,bhtsk->bhts', qc, kc, decay)
        o = o_inter + jnp.einsum('bhts,bhsv->bhtv', att, vc)
        last = cum[:, :, -1:, :]
        s_new = jnp.exp(last[:, :, 0, :])[..., None] * s + jnp.einsum('bhsk,bhsv->bhkv', kc * jnp.exp(last - cum), vc)
        return s_new, o

    s_fin, o = lax.scan(step, s0, (to_chunks(q), to_chunks(k), to_chunks(v), to_chunks(log_f)))
    o = jnp.moveaxis(o, 0, 2).reshape(b, h, n, v.shape[-1])
    return o, s_fin


def final_state(k, v, log_f):
    cum = jnp.cumsum(log_f.astype(jnp.float32), axis=2)
    w = jnp.exp(cum[:, :, -1:, :] - cum)
    return jnp.einsum('bhnk,bhnv->bhkv', k * w, v)


def box_mean(x, w, axis):
    length = x.shape[axis]
    cs = jnp.cumsum(x.astype(jnp.float32), axis=axis)
    zero = jnp.zeros_like(lax.slice_in_dim(cs, 0, 1, axis=axis))
    cs = jnp.concatenate([zero, cs], axis=axis)
    pos = jnp.arange(length)
    lo = jnp.clip(pos - w // 2, 0, length)
    hi = jnp.clip(pos - w // 2 + w, 0, length)
    s = jnp.take(cs, hi, axis=axis) - jnp.take(cs, lo, axis=axis)
    cnt = (hi - lo).astype(jnp.float32).reshape((length,) + (1,) * (x.ndim - axis - 1))
    return (s / cnt).astype(x.dtype)


def pool_branch(u, pool_w, pool_scale, on_grid):
    b, n, _ = u.shape
    if on_grid:
        rows = n // GRID_W
        u = u.reshape(b, rows, GRID_W, D_POOL)
    outs = []
    for gi, w in enumerate(POOL_WINDOWS):
        ug = u[..., gi * POOL_GROUP:(gi + 1) * POOL_GROUP]
        m = box_mean(ug, w, 1)
        if on_grid:
            m = box_mean(m, w, 2)
        outs.append(jnp.einsum('...c,ce->...e', m - ug, pool_w[gi]))
    y = jnp.concatenate(outs, axis=-1) * pool_scale
    return y.reshape(b, n, D_POOL)


def token_mixer(h, w_in, lb, o_gain, pool_w, pool_scale, w_br_rec, w_br_pool, w_out, s0_f, s0_b, on_grid):
    p = jnp.einsum('bnd,de->bne', h, w_in)
    q, f_fwd, f_bwd, i, g, u, ga, gb = jnp.split(p, SPLIT_IDX, axis=-1)
    k_f, lf_f, k_b, lf_b, v = recurrence_inputs(f_fwd, f_bwd, i, lb)
    qh = to_heads(jax.nn.silu(q)) * HGRN_SCALE
    o_f, s_f = gla_scan(qh, k_f, v, lf_f, s0_f)
    o_b, s_b = gla_scan(flip_seq(qh), flip_seq(k_b), flip_seq(v), flip_seq(lf_b), s0_b)
    o = from_heads(o_f + flip_seq(o_b))
    y_rec = (head_rms_norm(o, o_gain) * jax.nn.silu(g.astype(jnp.float32))).astype(h.dtype)
    y_pool = pool_branch(u, pool_w, pool_scale, on_grid)
    merged = (jax.nn.sigmoid(ga) * jnp.einsum('bnc,cd->bnd', y_rec, w_br_rec)
              + jax.nn.sigmoid(gb) * jnp.einsum('bnc,cd->bnd', y_pool, w_br_pool))
    return jnp.einsum('bnd,de->bne', merged, w_out), s_f, s_b


def context_states(h, w_in, lb):
    p = jnp.einsum('bnd,de->bne', h, w_in[:, D_HGRN:4 * D_HGRN])
    f_fwd, f_bwd, i = jnp.split(p, 3, axis=-1)
    k_f, lf_f, k_b, lf_b, v = recurrence_inputs(f_fwd, f_bwd, i, lb)
    s_f = final_state(k_f, v, lf_f)
    s_b = final_state(flip_seq(k_b), flip_seq(v), flip_seq(lf_b))
    return s_f, s_b


def swiglu(h, wg, wu, wd):
    return jnp.einsum('bnf,fd->bnd', jax.nn.silu(jnp.einsum('bnd,df->bnf', h, wg)) * jnp.einsum('bnd,df->bnf', h, wu), wd)


def moe_swiglu(h, router, wg, wu, wd):
    logits = jnp.einsum('bnd,de->bne', h, router).astype(jnp.float32)
    top_v, top_i = lax.top_k(logits, TOP_K)
    probs = jax.nn.softmax(top_v, axis=-1)
    gates = jnp.sum(jax.nn.one_hot(top_i, N_EXPERTS, dtype=jnp.float32) * probs[..., None], axis=-2).astype(h.dtype)
    y = jnp.zeros_like(h)
    for e in range(N_EXPERTS):
        y = y + gates[..., e:e + 1] * swiglu(h, wg[e], wu[e], wd[e])
    return y


def channel_mixer(h, l, ffn_w_gate, ffn_w_up, ffn_w_down, moe_router, moe_w_gate, moe_w_up, moe_w_down):
    if l % 2 == 0:
        j = l // 2
        return swiglu(h, ffn_w_gate[j], ffn_w_up[j], ffn_w_down[j])
    j = l // 2
    return moe_swiglu(h, moe_router[j], moe_w_gate[j], moe_w_up[j], moe_w_down[j])


def setup_inputs(seed: int = 0) -> dict:
    key = jax.random.key(seed)
    ks = jax.random.split(key, 32)
    f32 = jnp.float32

    def nrm(k, shape, scale):
        return jax.random.normal(k, shape, f32) * scale

    def gain(k, shape):
        return 1.0 + 0.05 * jax.random.normal(k, shape, f32)

    D = D_MODEL
    return {
        "x": nrm(ks[0], (BATCH, SEQ, D), 1.0),
        "c": nrm(ks[1], (BATCH, D), 1.0),
        "ctx": nrm(ks[2], (BATCH, CTX_LEN, D), 1.0),
        "c_ctx": nrm(ks[3], (D,), 1.0),
        "ada_w": nrm(ks[4], (DEPTH, D, 6 * D), D ** -0.5),
        "ada_b": nrm(ks[5], (DEPTH, 6 * D), 0.02),
        "mix_pre": gain(ks[6], (DEPTH, D)),
        "mix_post": gain(ks[7], (DEPTH, D)),
        "ffn_pre": gain(ks[8], (DEPTH, D)),
        "ffn_post": gain(ks[9], (DEPTH, D)),
        "w_in": nrm(ks[10], (DEPTH, D, IN_COLS), D ** -0.5),
        "hgrn_lb": nrm(ks[11], (DEPTH, 2, D_HGRN), 0.5),
        "hgrn_gain": gain(ks[12], (DEPTH, D_HGRN)),
        "pool_w": nrm(ks[13], (DEPTH, len(POOL_WINDOWS), POOL_GROUP, POOL_GROUP), POOL_GROUP ** -0.5),
        "pool_scale": gain(ks[14], (DEPTH, D_POOL)),
        "w_branch_rec": nrm(ks[15], (DEPTH, D_HGRN, D), D_HGRN ** -0.5),
        "w_branch_pool": nrm(ks[16], (DEPTH, D_POOL, D), D_POOL ** -0.5),
        "w_out": nrm(ks[17], (DEPTH, D, D), D ** -0.5),
        "ffn_w_gate": nrm(ks[18], (N_DENSE, D, D_FF), D ** -0.5),
        "ffn_w_up": nrm(ks[19], (N_DENSE, D, D_FF), D ** -0.5),
        "ffn_w_down": nrm(ks[20], (N_DENSE, D_FF, D), D_FF ** -0.5),
        "moe_router": nrm(ks[21], (N_MOE, D, N_EXPERTS), D ** -0.5),
        "moe_w_gate": nrm(ks[22], (N_MOE, N_EXPERTS, D, D_EXPERT), D ** -0.5),
        "moe_w_up": nrm(ks[23], (N_MOE, N_EXPERTS, D, D_EXPERT), D ** -0.5),
        "moe_w_down": nrm(ks[24], (N_MOE, N_EXPERTS, D_EXPERT, D), D_EXPERT ** -0.5),
    }


def reference(x, c, ctx, c_ctx, ada_w, ada_b, mix_pre, mix_post, ffn_pre, ffn_post, w_in, hgrn_lb, hgrn_gain,
              pool_w, pool_scale, w_branch_rec, w_branch_pool, w_out, ffn_w_gate, ffn_w_up, ffn_w_down,
              moe_router, moe_w_gate, moe_w_up, moe_w_down):
    lb_all = jnp.cumsum(jax.nn.softmax(hgrn_lb.astype(jnp.float32), axis=0), axis=0)
    lb_all = lb_all - lb_all[0:1]
    cx = ctx
    b = x.shape[0]
    for l in range(DEPTH):
        last = l == DEPTH - 1
        mod_x = (jnp.einsum('bd,de->be', jax.nn.silu(c), ada_w[l]) + ada_b[l])[:, None, :]
        sh1, sc1, g1, sh2, sc2, g2 = jnp.split(mod_x, 6, axis=-1)
        mod_c = jnp.einsum('d,de->e', jax.nn.silu(c_ctx), ada_w[l]) + ada_b[l]
        csh1, csc1, cg1, csh2, csc2, cg2 = jnp.split(mod_c, 6, axis=-1)
        mix_params = (w_in[l], lb_all[l], hgrn_gain[l], pool_w[l], pool_scale[l],
                      w_branch_rec[l], w_branch_pool[l], w_out[l])
        ffn_params = (ffn_w_gate, ffn_w_up, ffn_w_down, moe_router, moe_w_gate, moe_w_up, moe_w_down)

        hc = rms_norm(cx, mix_pre[l]) * (1.0 + csc1) + csh1
        if last:
            s_f, s_b = context_states(hc, w_in[l], lb_all[l])
        else:
            s0 = jnp.zeros((b, HGRN_HEADS, HGRN_DK, HGRN_DV), jnp.float32)
            yc, s_f, s_b = token_mixer(hc, *mix_params, s0, s0, False)
            cx = cx + cg1 * rms_norm(yc, mix_post[l])
        hx = rms_norm(x, mix_pre[l]) * (1.0 + sc1) + sh1
        yx, _, _ = token_mixer(hx, *mix_params, s_f, s_b, True)
        x = x + g1 * rms_norm(yx, mix_post[l])

        hx2 = rms_norm(x, ffn_pre[l]) * (1.0 + sc2) + sh2
        x = x + g2 * rms_norm(channel_mixer(hx2, l, *ffn_params), ffn_post[l])
        if not last:
            hc2 = rms_norm(cx, ffn_pre[l]) * (1.0 + csc2) + csh2
            cx = cx + cg2 * rms_norm(channel_mixer(hc2, l, *ffn_params), ffn_post[l])
    return x
```

```python
import functools

import jax
import jax.numpy as jnp
from jax import lax
from jax.experimental import pallas as pl
from jax.experimental.pallas import tpu as pltpu

F32 = jnp.float32
BF16 = jnp.bfloat16

EPS = 1e-6
LANES = 128
SUBLANES = 8
HEADS = 4
HEAD_DIM = 128
D_HGRN = HEADS * HEAD_DIM
D_POOL = 512
POOL_WINDOWS = (2, 4, 8, 16)
POOL_GROUP = D_POOL // len(POOL_WINDOWS)
GRID_W = 64
CHUNK = 64
DIAG = SUBLANES
TOP_K = 2
MOD_ROWS = 40
VMEM_LIMIT = 56 * 1024 * 1024


def _cparams(sem):
    return pltpu.CompilerParams(dimension_semantics=sem, vmem_limit_bytes=VMEM_LIMIT)


def _pick_tile(n, cap, mult):
    if n <= cap:
        return n
    best = None
    for t in range(mult, cap + 1, mult):
        if n % t == 0:
            best = t
    assert best is not None, (n, cap, mult)
    return best


def _sigmoid(z):
    return 1.0 / (1.0 + jnp.exp(-z))


def _silu(z):
    return z * _sigmoid(z)


def _mod_kernel(c_ref, w_ref, b_ref, o_ref):
    a = _silu(c_ref[...])
    o_ref[0] = jnp.dot(a, w_ref[0], preferred_element_type=F32,
                       precision=lax.Precision.HIGHEST) + b_ref[0]


def _modulation(cond, ada_w, ada_b):
    depth, d, six_d = ada_w.shape
    tn = _pick_tile(six_d, 1024, LANES)
    return pl.pallas_call(
        _mod_kernel,
        grid=(depth, six_d // tn),
        in_specs=[
            pl.BlockSpec((MOD_ROWS, d), lambda l, j: (0, 0)),
            pl.BlockSpec((1, d, tn), lambda l, j: (l, 0, j)),
            pl.BlockSpec((1, 1, tn), lambda l, j: (l, 0, j)),
        ],
        out_specs=pl.BlockSpec((1, MOD_ROWS, tn), lambda l, j: (l, 0, j)),
        out_shape=jax.ShapeDtypeStruct((depth, MOD_ROWS, six_d), F32),
        compiler_params=_cparams(("parallel", "parallel")),
        name="modulation",
    )(cond, ada_w, ada_b.reshape(depth, 1, six_d))


def _rms(x, gain):
    ms = jnp.mean(x * x, axis=-1, keepdims=True)
    return x * lax.rsqrt(ms + EPS) * gain


def _inproj_kernel(x_ref, mod_ref, gain_ref, w_ref, o_ref, *, tn):
    m = mod_ref[0]
    h = _rms(x_ref[0], gain_ref[...]) * (1.0 + m[1:2]) + m[0:1]
    hb = h.astype(BF16)
    for j in range(w_ref.shape[1] // tn):
        o_ref[0, :, j * tn:(j + 1) * tn] = jnp.dot(
            hb, w_ref[:, j * tn:(j + 1) * tn], preferred_element_type=F32).astype(o_ref.dtype)


def _in_proj(xs, mod_l, mod_row, gain, w):
    b, n, d = xs.shape
    nc = w.shape[1]
    tm = _pick_tile(n, 512, SUBLANES)
    tn = _pick_tile(nc, 512, LANES)
    if mod_row is None:
        mod_map = lambda bi, i: (bi, 0, 0)
    else:
        mod_map = lambda bi, i: (mod_row, 0, 0)
    return pl.pallas_call(
        functools.partial(_inproj_kernel, tn=tn),
        grid=(b, n // tm),
        in_specs=[
            pl.BlockSpec((1, tm, d), lambda bi, i: (bi, i, 0)),
            pl.BlockSpec((1, 6, d), mod_map),
            pl.BlockSpec((1, d), lambda bi, i: (0, 0)),
            pl.BlockSpec((d, nc), lambda bi, i: (0, 0)),
        ],
        out_specs=pl.BlockSpec((1, tm, nc), lambda bi, i: (bi, i, 0)),
        out_shape=jax.ShapeDtypeStruct((b, n, nc), BF16),
        compiler_params=_cparams(("parallel", "parallel")),
        name="in_proj",
    )(xs, mod_l, gain.reshape(1, d), w)


def _dot_nt(a, b):
    return lax.dot_general(a, b, (((1,), (1,)), ((), ())), preferred_element_type=F32)


def _dot_tn(a, b):
    return lax.dot_general(a, b, (((0,), (0,)), ((), ())), preferred_element_type=F32)


def _gla_kernel(*refs, n, with_out):
    if with_out:
        (q_ref, ff_ref, fb_ref, i_ref, lb_ref, s0f_ref, s0b_ref,
         o_ref, sf_ref, sb_ref, qs, vs, lf, kk) = refs
    else:
        (ff_ref, fb_ref, i_ref, lb_ref, s0f_ref, s0b_ref,
         sf_ref, sb_ref, vs, lf, kk) = refs
    c = CHUNK
    nchunks = n // c
    nblk = c // DIAG

    if with_out:
        qs[...] = _silu(q_ref[0].astype(F32)) * (HEAD_DIM ** -0.5)
    vs[...] = i_ref[0].astype(F32)

    row = lax.broadcasted_iota(jnp.int32, (c, c), 0)
    col = lax.broadcasted_iota(jnp.int32, (c, c), 1)
    rowv = lax.broadcasted_iota(jnp.int32, (c, HEAD_DIM), 0)
    row3 = lax.broadcasted_iota(jnp.int32, (nblk, DIAG, HEAD_DIM), 1)

    for rev in (False, True):
        z = (fb_ref if rev else ff_ref)[0].astype(F32)
        lb = lb_ref[1:2, :] if rev else lb_ref[0:1, :]
        e = jnp.exp(-jnp.abs(z))
        log_sig = jnp.minimum(z, 0.0) - jnp.log(1.0 + e)
        a = jnp.log(lb)
        bt = jnp.log(1.0 - lb) + log_sig
        mx = jnp.maximum(a, bt)
        mn = jnp.minimum(a, bt)
        lf[...] = mx + jnp.log(1.0 + jnp.exp(mn - mx))
        kk[...] = (1.0 - lb) * (jnp.where(z >= 0.0, e, 1.0) / (1.0 + e))

        tri = (jnp.where(col >= row, 1.0, 0.0) if rev else jnp.where(col <= row, 1.0, 0.0)).astype(BF16)

        def body(ci, st, rev=rev, tri=tri):
            cidx = (nchunks - 1 - ci) if rev else ci
            r0 = pl.multiple_of(cidx * c, c)
            g = lf[pl.ds(r0, c), :]
            g_hi = g.astype(BF16)
            g_lo = (g - g_hi.astype(F32)).astype(BF16)
            cum = (jnp.dot(tri, g_hi, preferred_element_type=F32)
                   + jnp.dot(tri, g_lo, preferred_element_type=F32))
            kc = kk[pl.ds(r0, c), :]
            vc = vs[pl.ds(r0, c), :]
            vcb = vc.astype(BF16)
            tot = cum[0:1, :] if rev else cum[c - 1:c, :]

            if with_out:
                qc = qs[pl.ds(r0, c), :]
                o = _dot_nt((qc * jnp.exp(cum)).astype(BF16), st.astype(BF16))
                att = jnp.zeros((c, c), F32)
                m = c
                while m > DIAG:
                    half = m // 2
                    pieces = []
                    for blk in range(c // m):
                        rr = blk * m + (half if rev else half - 1)
                        pieces.append(jnp.broadcast_to(cum[rr:rr + 1, :], (m, HEAD_DIM)))
                    ref = pieces[0] if len(pieces) == 1 else jnp.concatenate(pieces, axis=0)
                    in_right = (rowv % m) >= half
                    t_rows = jnp.logical_not(in_right) if rev else in_right
                    qt = jnp.where(t_rows, qc * jnp.exp(jnp.minimum(cum - ref, 0.0)), 0.0)
                    kt = jnp.where(t_rows, 0.0, kc * jnp.exp(jnp.minimum(ref - cum, 0.0)))
                    am = _dot_nt(qt.astype(BF16), kt.astype(BF16))
                    if m < c:
                        am = jnp.where((row // m) == (col // m), am, 0.0)
                    att = att + am
                    m = half
                o = o + jnp.dot(att.astype(BF16), vcb, preferred_element_type=F32)

                q3 = qc.reshape(nblk, DIAG, HEAD_DIM)
                k3 = kc.reshape(nblk, DIAG, HEAD_DIM)
                v3 = vc.reshape(nblk, DIAG, HEAD_DIM)
                c3 = cum.reshape(nblk, DIAG, HEAD_DIM)
                acc3 = jnp.zeros((nblk, DIAG, HEAD_DIM), F32)
                for j in range(DIAG):
                    dec = jnp.exp(jnp.minimum(c3 - c3[:, j:j + 1, :], 0.0))
                    w = q3 * k3[:, j:j + 1, :] * dec
                    w = jnp.where((row3 <= j) if rev else (row3 >= j), w, 0.0)
                    acc3 = acc3 + jnp.sum(w, axis=-1, keepdims=True) * v3[:, j:j + 1, :]
                o = o + acc3.reshape(c, HEAD_DIM)
                if rev:
                    o_ref[0, pl.ds(r0, c), :] = o_ref[0, pl.ds(r0, c), :] + o
                else:
                    o_ref[0, pl.ds(r0, c), :] = o

            khat = kc * jnp.exp(tot - cum)
            return jnp.exp(tot) * st + _dot_tn(vcb, khat.astype(BF16))

        s0_ref = s0b_ref if rev else s0f_ref
        s_fin = lax.fori_loop(0, nchunks, body, s0_ref[0, 0])
        (sb_ref if rev else sf_ref)[0, 0] = s_fin


def _gla(p, lb, s0f, s0b, *, with_out, col0):
    b, n, _ = p.shape
    assert n % CHUNK == 0
    hb = D_HGRN // HEAD_DIM

    def col_spec(k):
        return pl.BlockSpec((1, n, HEAD_DIM), lambda bi, h, k=k: (bi, 0, col0 + k * hb + h))

    st_spec = pl.BlockSpec((1, 1, HEAD_DIM, HEAD_DIM), lambda bi, h: (bi, h, 0, 0))
    n_cols = 4 if with_out else 3
    in_specs = [col_spec(k) for k in range(n_cols)] + [
        pl.BlockSpec((2, HEAD_DIM), lambda bi, h: (0, h)), st_spec, st_spec]
    st_shape = jax.ShapeDtypeStruct((b, HEADS, HEAD_DIM, HEAD_DIM), F32)
    out_specs = [st_spec, st_spec]
    out_shape = [st_shape, st_shape]
    scratch = [pltpu.VMEM((n, HEAD_DIM), F32) for _ in range(3)]
    if with_out:
        out_specs = [pl.BlockSpec((1, n, HEAD_DIM), lambda bi, h: (bi, 0, h))] + out_specs
        out_shape = [jax.ShapeDtypeStruct((b, n, D_HGRN), F32)] + out_shape
        scratch = [pltpu.VMEM((n, HEAD_DIM), F32)] + scratch
    return pl.pallas_call(
        functools.partial(_gla_kernel, n=n, with_out=with_out),
        grid=(b, HEADS),
        in_specs=in_specs,
        out_specs=out_specs,
        out_shape=out_shape,
        scratch_shapes=scratch,
        compiler_params=_cparams(("parallel", "parallel")),
        name="gla" if with_out else "gla_state",
    )(*([p] * n_cols), lb, s0f, s0b)


def _pool_kernel(u_ref, pw_ref, ps_ref, o_ref, *, n, width):
    rows = n // width
    t = lax.broadcasted_iota(jnp.int32, (n, POOL_GROUP), 0)
    colp = t % width
    rowp = t // width

    def shift(a, delta, pos, length, stride):
        rolled = pltpu.roll(a, (delta * stride) % n, 0)
        ok = jnp.logical_and(pos - delta >= 0, pos - delta < length)
        return jnp.where(ok, rolled, 0.0)

    def box_sum(a, w, pos, length, stride):
        trail, lead = a, a
        h = 1
        while h < w // 2:
            trail = trail + shift(trail, h, pos, length, stride)
            lead = lead + shift(lead, -h, pos, length, stride)
            h *= 2
        return shift(trail, 1, pos, length, stride) + lead

    def count(pos, w, length):
        lo = jnp.clip(pos - w // 2, 0, length)
        hi = jnp.clip(pos - w // 2 + w, 0, length)
        return (hi - lo).astype(F32)

    for gi, w in enumerate(POOL_WINDOWS):
        ug = u_ref[0, :, gi * POOL_GROUP:(gi + 1) * POOL_GROUP].astype(F32)
        if rows > 1:
            s = box_sum(ug, w, rowp, rows, width) / count(rowp, w, rows)
            s = box_sum(s, w, colp, width, 1) / count(colp, w, width)
        else:
            s = box_sum(ug, w, colp, width, 1) / count(colp, w, width)
        y = jnp.dot((s - ug).astype(BF16), pw_ref[gi], preferred_element_type=F32)
        y = y * ps_ref[:, gi * POOL_GROUP:(gi + 1) * POOL_GROUP]
        o_ref[0, :, gi * POOL_GROUP:(gi + 1) * POOL_GROUP] = y.astype(o_ref.dtype)


def _pool(p, pool_w, pool_scale, *, on_grid, col_block):
    b, n, _ = p.shape
    width = GRID_W if on_grid else n
    assert n % width == 0 and width & (width - 1) == 0
    return pl.pallas_call(
        functools.partial(_pool_kernel, n=n, width=width),
        grid=(b,),
        in_specs=[
            pl.BlockSpec((1, n, D_POOL), lambda bi: (bi, 0, col_block)),
            pl.BlockSpec(pool_w.shape, lambda bi: (0, 0, 0)),
            pl.BlockSpec((1, D_POOL), lambda bi: (0, 0)),
        ],
        out_specs=pl.BlockSpec((1, n, D_POOL), lambda bi: (bi, 0, 0)),
        out_shape=jax.ShapeDtypeStruct((b, n, D_POOL), BF16),
        compiler_params=_cparams(("parallel",)),
        name="pool",
    )(p, pool_w, pool_scale.reshape(1, D_POOL))


def _merge_kernel(*refs, moe, n_experts):
    (o_ref, g_ref, yp_ref, ga_ref, gb_ref, x_ref, mod_ref, og_ref, post_ref, pre2_ref,
     wr_ref, wp_ref, wo_ref) = refs[:13]
    if moe:
        router_ref, xn_ref, hrow_ref, route_ref = refs[13:]
    else:
        xn_ref, h2_ref = refs[13:]
    tm = x_ref.shape[1]
    m = mod_ref[0]
    o = o_ref[0]
    parts = []
    for h in range(HEADS):
        oh = o[:, h * HEAD_DIM:(h + 1) * HEAD_DIM]
        parts.append(oh * lax.rsqrt(jnp.mean(oh * oh, axis=-1, keepdims=True) + EPS))
    on = jnp.concatenate(parts, axis=-1) * og_ref[...]
    y_rec = (on * _silu(g_ref[0].astype(F32))).astype(BF16)
    rec = jnp.dot(y_rec, wr_ref[...], preferred_element_type=F32)
    pool = jnp.dot(yp_ref[0], wp_ref[...], preferred_element_type=F32)
    merged = _sigmoid(ga_ref[0].astype(F32)) * rec + _sigmoid(gb_ref[0].astype(F32)) * pool
    y = jnp.dot(merged.astype(BF16), wo_ref[...], preferred_element_type=F32)
    xn = x_ref[0] + m[2:3] * _rms(y, post_ref[...])
    xn_ref[0] = xn
    h2 = _rms(xn, pre2_ref[...]) * (1.0 + m[4:5]) + m[3:4]
    if not moe:
        h2_ref[0] = h2.astype(h2_ref.dtype)
        return
    d = h2.shape[-1]
    for s in range(d // LANES):
        hrow_ref[pl.ds(s, tm, stride=d // LANES), :] = h2[:, s * LANES:(s + 1) * LANES]
    logits = jnp.dot(h2, router_ref[...], preferred_element_type=F32,
                     precision=lax.Precision.HIGHEST)
    lane = lax.broadcasted_iota(jnp.int32, logits.shape, 1)
    neg = jnp.float32(-jnp.inf)
    lg = jnp.where(lane < n_experts, logits, neg)
    m1 = jnp.max(lg, axis=-1, keepdims=True)
    i1 = jnp.min(jnp.where(lg == m1, lane, LANES), axis=-1, keepdims=True)
    lg2 = jnp.where(lane == i1, neg, lg)
    m2 = jnp.max(lg2, axis=-1, keepdims=True)
    i2 = jnp.min(jnp.where(lg2 == m2, lane, LANES), axis=-1, keepdims=True)
    e2 = jnp.exp(m2 - m1)
    p1 = 1.0 / (1.0 + e2)
    p2 = e2 * p1
    route = jnp.where(lane == 0, i1.astype(F32),
                      jnp.where(lane == 1, i2.astype(F32),
                                jnp.where(lane == 2, p1, jnp.where(lane == 3, p2, 0.0))))
    route_ref[...] = route


def _merge(o, p, ypool, xs, mod_l, mod_row, o_gain, post_gain, pre2_gain, w_rec, w_pool, w_out,
           router=None):
    b, n, d = xs.shape
    moe = router is not None
    tm = _pick_tile(n, 512, SUBLANES)
    nt = n // tm
    gate_blk = d // D_HGRN * 0 + 1
    del gate_blk
    if mod_row is None:
        mod_map = lambda bi, i: (bi, 0, 0)
    else:
        mod_map = lambda bi, i: (mod_row, 0, 0)
    g_blk = 4 * D_HGRN // D_HGRN
    ga_col = 5 * D_HGRN + D_POOL
    assert ga_col % d == 0
    ga_blk = ga_col // d
    const = lambda bi, i: (0, 0)
    in_specs = [
        pl.BlockSpec((1, tm, D_HGRN), lambda bi, i: (bi, i, 0)),
        pl.BlockSpec((1, tm, D_HGRN), lambda bi, i: (bi, i, g_blk)),
        pl.BlockSpec((1, tm, D_POOL), lambda bi, i: (bi, i, 0)),
        pl.BlockSpec((1, tm, d), lambda bi, i: (bi, i, ga_blk)),
        pl.BlockSpec((1, tm, d), lambda bi, i: (bi, i, ga_blk + 1)),
        pl.BlockSpec((1, tm, d), lambda bi, i: (bi, i, 0)),
        pl.BlockSpec((1, 6, d), mod_map),
        pl.BlockSpec((1, D_HGRN), const),
        pl.BlockSpec((1, d), const),
        pl.BlockSpec((1, d), const),
        pl.BlockSpec((D_HGRN, d), const),
        pl.BlockSpec((D_POOL, d), const),
        pl.BlockSpec((d, d), const),
    ]
    args = [o, p, ypool, p, p, xs, mod_l, o_gain.reshape(1, D_HGRN), post_gain.reshape(1, d),
            pre2_gain.reshape(1, d), w_rec, w_pool, w_out]
    out_specs = [pl.BlockSpec((1, tm, d), lambda bi, i: (bi, i, 0))]
    out_shape = [jax.ShapeDtypeStruct((b, n, d), F32)]
    n_experts = 0
    if moe:
        n_experts = router.shape[1]
        router_pad = jnp.zeros((d, LANES), F32).at[:, :n_experts].set(router)
        in_specs.append(pl.BlockSpec((d, LANES), const))
        args.append(router_pad)
        sub = d // LANES
        out_specs += [pl.BlockSpec((tm * sub, LANES), lambda bi, i: (bi * nt + i, 0)),
                      pl.BlockSpec((tm, LANES), lambda bi, i: (bi * nt + i, 0))]
        out_shape += [jax.ShapeDtypeStruct((b * n * sub, LANES), F32),
                      jax.ShapeDtypeStruct((b * n, LANES), F32)]
    else:
        out_specs.append(pl.BlockSpec((1, tm, d), lambda bi, i: (bi, i, 0)))
        out_shape.append(jax.ShapeDtypeStruct((b, n, d), BF16))
    return pl.pallas_call(
        functools.partial(_merge_kernel, moe=moe, n_experts=n_experts),
        grid=(b, nt),
        in_specs=in_specs,
        out_specs=out_specs,
        out_shape=out_shape,
        compiler_params=_cparams(("parallel", "parallel")),
        name="merge_moe" if moe else "merge",
    )(*args)


def _ffn_kernel(h_ref, wg_ref, wu_ref, wd_ref, x_ref, mod_ref, post_ref, o_ref, acc):
    j = pl.program_id(1)

    @pl.when(j == 0)
    def _():
        acc[...] = jnp.zeros_like(acc)

    h = h_ref[...]
    g = jnp.dot(h, wg_ref[...], preferred_element_type=F32)
    u = jnp.dot(h, wu_ref[...], preferred_element_type=F32)
    acc[...] += jnp.dot((_silu(g) * u).astype(BF16), wd_ref[...], preferred_element_type=F32)

    @pl.when(j == pl.num_programs(1) - 1)
    def _():
        m = mod_ref[0]
        o_ref[...] = x_ref[...] + m[5:6] * _rms(acc[...], post_ref[...])


def _ffn_dense(h2, xs, mod_l, mod_row, post_gain, wg, wu, wd):
    b, n, d = xs.shape
    f = wg.shape[1]
    r = b * n
    tm = _pick_tile(n, 512, SUBLANES)
    tf = _pick_tile(f, 1408, LANES)
    tiles_per_batch = n // tm
    if mod_row is None:
        mod_map = lambda i, j: (i // tiles_per_batch, 0, 0)
    else:
        mod_map = lambda i, j: (mod_row, 0, 0)
    out = pl.pallas_call(
        _ffn_kernel,
        grid=(r // tm, f // tf),
        in_specs=[
            pl.BlockSpec((tm, d), lambda i, j: (i, 0)),
            pl.BlockSpec((d, tf), lambda i, j: (0, j)),
            pl.BlockSpec((d, tf), lambda i, j: (0, j)),
            pl.BlockSpec((tf, d), lambda i, j: (j, 0)),
            pl.BlockSpec((tm, d), lambda i, j: (i, 0)),
            pl.BlockSpec((1, 6, d), mod_map),
            pl.BlockSpec((1, d), lambda i, j: (0, 0)),
        ],
        out_specs=pl.BlockSpec((tm, d), lambda i, j: (i, 0)),
        out_shape=jax.ShapeDtypeStruct((r, d), F32),
        scratch_shapes=[pltpu.VMEM((tm, d), F32)],
        compiler_params=_cparams(("parallel", "arbitrary")),
        name="ffn_dense",
    )(h2.reshape(r, d), wg, wu, wd, xs.reshape(r, d), mod_l, post_gain.reshape(1, d))
    return out.reshape(b, n, d)


def _moe_kernel(te_ref, tv_ref, idx_ref, idx_next_ref, h_hbm, wg_ref, wu_ref, wd_ref, o_ref,
                xg, xb, acc, sem, *, tm, sub):
    i = pl.program_id(0)
    j = pl.program_id(1)
    n_tiles = pl.num_programs(0)
    slot = i % 2

    def row_copy(tok, dst_slot, r):
        src = h_hbm.at[pl.ds(pl.multiple_of(tok * sub, sub), sub), :]
        dst = xg.at[dst_slot, pl.ds(pl.multiple_of(r * sub, sub), sub), :]
        return pltpu.make_async_copy(src, dst, sem.at[dst_slot])

    def start_gather(ids_ref, dst_slot):
        def issue(r, carry):
            row_copy(ids_ref[0, 0, r], dst_slot, r).start()
            return carry
        lax.fori_loop(0, tm, issue, 0)

    @pl.when(j == 0)
    def _():
        @pl.when(i == 0)
        def _():
            start_gather(idx_ref, 0)

        pltpu.make_async_copy(h_hbm.at[pl.ds(0, tm * sub), :], xg.at[slot], sem.at[slot]).wait()

        @pl.when(i + 1 < n_tiles)
        def _():
            start_gather(idx_next_ref, 1 - slot)

        for s in range(sub):
            xb[:, s * LANES:(s + 1) * LANES] = xg[slot, pl.ds(s, tm, stride=sub), :].astype(BF16)
        acc[...] = jnp.zeros_like(acc)

    @pl.when(tv_ref[i] > 0)
    def _():
        h = xb[...]
        g = jnp.dot(h, wg_ref[0], preferred_element_type=F32)
        u = jnp.dot(h, wu_ref[0], preferred_element_type=F32)
        acc[...] += jnp.dot((_silu(g) * u).astype(BF16), wd_ref[0], preferred_element_type=F32)

    @pl.when(j == pl.num_programs(1) - 1)
    def _():
        for s in range(sub):
            o_ref[pl.ds(s, tm, stride=sub), :] = acc[:, s * LANES:(s + 1) * LANES]


def _combine_kernel(pos_ref, y_hbm, route_ref, x_ref, mod_ref, post_ref, o_ref, yb, sem,
                    *, tm, sub):
    def row_copy(a):
        src = y_hbm.at[pl.ds(pl.multiple_of(pos_ref[0, 0, a] * sub, sub), sub), :]
        dst = yb.at[pl.ds(pl.multiple_of(a * sub, sub), sub), :]
        return pltpu.make_async_copy(src, dst, sem.at[0])

    def issue(a, carry):
        row_copy(a).start()
        return carry

    lax.fori_loop(0, TOP_K * tm, issue, 0)
    pltpu.make_async_copy(y_hbm.at[pl.ds(0, TOP_K * tm * sub), :], yb, sem.at[0]).wait()

    rt = route_ref[...]
    y1 = jnp.concatenate([yb[pl.ds(s, tm, stride=sub), :] for s in range(sub)], axis=-1)
    y2 = jnp.concatenate([yb[pl.ds(tm * sub + s, tm, stride=sub), :] for s in range(sub)], axis=-1)
    y = rt[:, 2:3] * y1 + rt[:, 3:4] * y2
    m = mod_ref[0]
    o_ref[...] = x_ref[...] + m[5:6] * _rms(y, post_ref[...])


def _ffn_moe(hrows, route, xs, mod_l, mod_row, post_gain, wg, wu, wd):
    b, n, d = xs.shape
    r = b * n
    n_exp, _, f = wg.shape
    sub = d // LANES
    tm = _pick_tile(r, 1024, SUBLANES)
    tf = _pick_tile(f, 512, LANES)
    n_tiles = TOP_K * r // tm + n_exp
    assert (TOP_K * r) % tm == 0

    e_flat = route[:, :TOP_K].astype(jnp.int32).reshape(-1)
    onehot = (e_flat[:, None] == jnp.arange(n_exp, dtype=jnp.int32)[None, :]).astype(jnp.int32)
    csum = jnp.cumsum(onehot, axis=0)
    rank = jnp.sum((csum - onehot) * onehot, axis=1)
    counts = csum[-1]
    tiles_e = (counts + tm - 1) // tm
    tile_end = jnp.cumsum(tiles_e)
    tile_start = tile_end - tiles_e
    pos = tile_start[e_flat] * tm + rank
    src = jnp.zeros((n_tiles * tm,), jnp.int32).at[pos].set(
        jnp.arange(TOP_K * r, dtype=jnp.int32) // TOP_K, unique_indices=True)
    tile_ids = jnp.arange(n_tiles, dtype=jnp.int32)
    tile_valid = (tile_ids < tile_end[-1]).astype(jnp.int32)
    tile_expert = jnp.minimum(jnp.searchsorted(tile_end, tile_ids, side="right"),
                              n_exp - 1).astype(jnp.int32)
    src3 = src.reshape(n_tiles, 1, tm)

    nf = f // tf
    ysorted = pl.pallas_call(
        functools.partial(_moe_kernel, tm=tm, sub=sub),
        grid_spec=pltpu.PrefetchScalarGridSpec(
            num_scalar_prefetch=2,
            grid=(n_tiles, nf),
            in_specs=[
                pl.BlockSpec((1, 1, tm), lambda i, j, te, tv: (i, 0, 0),
                             memory_space=pltpu.MemorySpace.SMEM),
                pl.BlockSpec((1, 1, tm), lambda i, j, te, tv: (jnp.minimum(i + 1, n_tiles - 1), 0, 0),
                             memory_space=pltpu.MemorySpace.SMEM),
                pl.BlockSpec(memory_space=pl.ANY),
                pl.BlockSpec((1, d, tf), lambda i, j, te, tv: (te[i], 0, j)),
                pl.BlockSpec((1, d, tf), lambda i, j, te, tv: (te[i], 0, j)),
                pl.BlockSpec((1, tf, d), lambda i, j, te, tv: (te[i], j, 0)),
            ],
            out_specs=pl.BlockSpec((tm * sub, LANES), lambda i, j, te, tv: (i, 0)),
            scratch_shapes=[
                pltpu.VMEM((2, tm * sub, LANES), F32),
                pltpu.VMEM((tm, d), BF16),
                pltpu.VMEM((tm, d), F32),
                pltpu.SemaphoreType.DMA((2,)),
            ],
        ),
        out_shape=jax.ShapeDtypeStruct((n_tiles * tm * sub, LANES), F32),
        compiler_params=_cparams(("arbitrary", "arbitrary")),
        name="moe_experts",
    )(tile_expert, tile_valid, src3, src3, hrows, wg, wu, wd)

    tmc = _pick_tile(n, 512, SUBLANES)
    tiles_per_batch = n // tmc
    if mod_row is None:
        mod_map = lambda i: (i // tiles_per_batch, 0, 0)
    else:
        mod_map = lambda i: (mod_row, 0, 0)
    pos3 = pos.reshape(r // tmc, tmc, TOP_K).transpose(0, 2, 1).reshape(r // tmc, 1, TOP_K * tmc)
    out = pl.pallas_call(
        functools.partial(_combine_kernel, tm=tmc, sub=sub),
        grid=(r // tmc,),
        in_specs=[
            pl.BlockSpec((1, 1, TOP_K * tmc), lambda i: (i, 0, 0),
                         memory_space=pltpu.MemorySpace.SMEM),
            pl.BlockSpec(memory_space=pl.ANY),
            pl.BlockSpec((tmc, LANES), lambda i: (i, 0)),
            pl.BlockSpec((tmc, d), lambda i: (i, 0)),
            pl.BlockSpec((1, 6, d), mod_map),
            pl.BlockSpec((1, d), lambda i: (0, 0)),
        ],
        out_specs=pl.BlockSpec((tmc, d), lambda i: (i, 0)),
        out_shape=jax.ShapeDtypeStruct((r, d), F32),
        scratch_shapes=[
            pltpu.VMEM((TOP_K * tmc * sub, LANES), F32),
            pltpu.SemaphoreType.DMA((1,)),
        ],
        compiler_params=_cparams(("arbitrary",)),
        name="moe_combine",
    )(pos3, ysorted, route, xs.reshape(r, d), mod_l, post_gain.reshape(1, d))
    return out.reshape(b, n, d)


def kernel(x, c, ctx, c_ctx, ada_w, ada_b, mix_pre, mix_post, ffn_pre, ffn_post, w_in, hgrn_lb, hgrn_gain,
           pool_w, pool_scale, w_branch_rec, w_branch_pool, w_out, ffn_w_gate, ffn_w_up, ffn_w_down,
           moe_router, moe_w_gate, moe_w_up, moe_w_down):
    b, n, d = x.shape
    depth = ada_w.shape[0]
    assert b + 1 <= MOD_ROWS and d % LANES == 0
    ctx_row = b

    cond = jnp.zeros((MOD_ROWS, d), F32).at[:b].set(c).at[b].set(c_ctx)
    mod = _modulation(cond, ada_w, ada_b).reshape(depth, MOD_ROWS, 6, d)

    lb_all = jnp.cumsum(jax.nn.softmax(hgrn_lb.astype(F32), axis=0), axis=0)
    lb_all = lb_all - lb_all[0:1]

    bf = lambda a: a.astype(BF16)
    w_in_b, pool_w_b = bf(w_in), bf(pool_w)
    w_rec_b, w_pool_b, w_out_b = bf(w_branch_rec), bf(w_branch_pool), bf(w_out)
    ffn_g, ffn_u, ffn_d = bf(ffn_w_gate), bf(ffn_w_up), bf(ffn_w_down)
    moe_g, moe_u, moe_d = bf(moe_w_gate), bf(moe_w_up), bf(moe_w_down)

    u_blk = (5 * D_HGRN) // D_POOL
    s_zero = jnp.zeros((b, HEADS, HEAD_DIM, HEAD_DIM), F32)

    def mixer(xs, l, mod_row, s0f, s0b, on_grid, router):
        p = _in_proj(xs, mod[l], mod_row, mix_pre[l], w_in_b[l])
        o, s_f, s_b = _gla(p, lb_all[l], s0f, s0b, with_out=True, col0=0)
        yp = _pool(p, pool_w_b[l], pool_scale[l], on_grid=on_grid, col_block=u_blk)
        outs = _merge(o, p, yp, xs, mod[l], mod_row, hgrn_gain[l], mix_post[l], ffn_pre[l],
                      w_rec_b[l], w_pool_b[l], w_out_b[l], router=router)
        return outs, s_f, s_b

    def channel(outs, l, mod_row):
        j = l // 2
        if l % 2 == 0:
            xn, h2 = outs
            return _ffn_dense(h2, xn, mod[l], mod_row, ffn_post[l], ffn_g[j], ffn_u[j], ffn_d[j])
        xn, hrows, route = outs
        return _ffn_moe(hrows, route, xn, mod[l], mod_row, ffn_post[l], moe_g[j], moe_u[j], moe_d[j])

    cx = ctx
    for l in range(depth):
        last = l == depth - 1
        router = moe_router[l // 2] if l % 2 == 1 else None
        if last:
            pc = _in_proj(cx, mod[l], ctx_row, mix_pre[l], w_in_b[l][:, D_HGRN:4 * D_HGRN])
            s_f, s_b = _gla(pc, lb_all[l], s_zero, s_zero, with_out=False, col0=0)
            c_outs = None
        else:
            c_outs, s_f, s_b = mixer(cx, l, ctx_row, s_zero, s_zero, False, router)
        x_outs, _, _ = mixer(x, l, None, s_f, s_b, True, router)
        x = channel(x_outs, l, None)
        if not last:
            cx = channel(c_outs, l, ctx_row)
    return x
```

```python
import functools

import jax
import jax.numpy as jnp
from jax import lax
from jax.experimental import pallas as pl
from jax.experimental.pallas import tpu as pltpu

F32 = jnp.float32
BF16 = jnp.bfloat16

EPS = 1e-6
LANES = 128
SUBLANES = 8
HEADS = 4
HEAD_DIM = 128
D_HGRN = HEADS * HEAD_DIM
D_POOL = 512
POOL_WINDOWS = (2, 4, 8, 16)
POOL_GROUP = D_POOL // len(POOL_WINDOWS)
GRID_W = 64
CHUNK = 128
DIAG = SUBLANES
TOP_K = 2
MOD_ROWS = 40
VMEM_LIMIT = 56 * 1024 * 1024

def _cparams(sem):
    return pltpu.CompilerParams(dimension_semantics=sem, vmem_limit_bytes=VMEM_LIMIT)


def _pick_tile(n, cap, mult):
    if n <= cap:
        return n
    best = None
    for t in range(mult, cap + 1, mult):
        if n % t == 0:
            best = t
    assert best is not None, (n, cap, mult)
    return best


def _sigmoid(z):
    return 1.0 / (1.0 + jnp.exp(-z))


def _silu(z):
    return z * _sigmoid(z)


def _mod_kernel(c_ref, w_ref, b_ref, o_ref):
    a = _silu(c_ref[...])
    o_ref[0] = jnp.dot(a, w_ref[0], preferred_element_type=F32,
                       precision=lax.Precision.HIGHEST) + b_ref[0]


def _modulation(cond, ada_w, ada_b):
    depth, d, six_d = ada_w.shape
    tn = _pick_tile(six_d, 1024, LANES)
    return pl.pallas_call(
        _mod_kernel,
        grid=(depth, six_d // tn),
        in_specs=[
            pl.BlockSpec((MOD_ROWS, d), lambda l, j: (0, 0)),
            pl.BlockSpec((1, d, tn), lambda l, j: (l, 0, j)),
            pl.BlockSpec((1, 1, tn), lambda l, j: (l, 0, j)),
        ],
        out_specs=pl.BlockSpec((1, MOD_ROWS, tn), lambda l, j: (l, 0, j)),
        out_shape=jax.ShapeDtypeStruct((depth, MOD_ROWS, six_d), F32),
        compiler_params=_cparams(("parallel", "parallel")),
        name="modulation",
    )(cond, ada_w, ada_b.reshape(depth, 1, six_d))


def _rms(x, gain):
    ms = jnp.mean(x * x, axis=-1, keepdims=True)
    return x * lax.rsqrt(ms + EPS) * gain


def _inproj_kernel(x_ref, mod_ref, gain_ref, w_ref, o_ref, *, tn):
    m = mod_ref[0]
    h = _rms(x_ref[0], gain_ref[...]) * (1.0 + m[1:2]) + m[0:1]
    hb = h.astype(BF16)
    for j in range(w_ref.shape[1] // tn):
        o_ref[0, :, j * tn:(j + 1) * tn] = jnp.dot(
            hb, w_ref[:, j * tn:(j + 1) * tn], preferred_element_type=F32).astype(o_ref.dtype)


def _in_proj(xs, mod_l, mod_row, gain, w):
    b, n, d = xs.shape
    nc = w.shape[1]
    tm = _pick_tile(n, 512, SUBLANES)
    tn = _pick_tile(nc, 512, LANES)
    if mod_row is None:
        mod_map = lambda bi, i: (bi, 0, 0)
    else:
        mod_map = lambda bi, i: (mod_row, 0, 0)
    return pl.pallas_call(
        functools.partial(_inproj_kernel, tn=tn),
        grid=(b, n // tm),
        in_specs=[
            pl.BlockSpec((1, tm, d), lambda bi, i: (bi, i, 0)),
            pl.BlockSpec((1, 6, d), mod_map),
            pl.BlockSpec((1, d), lambda bi, i: (0, 0)),
            pl.BlockSpec((d, nc), lambda bi, i: (0, 0)),
        ],
        out_specs=pl.BlockSpec((1, tm, nc), lambda bi, i: (bi, i, 0)),
        out_shape=jax.ShapeDtypeStruct((b, n, nc), BF16),
        compiler_params=_cparams(("parallel", "parallel")),
        name="in_proj",
    )(xs, mod_l, gain.reshape(1, d), w)


def _neg_abs(a):
    bits = lax.bitcast_convert_type(a, jnp.uint32) | jnp.uint32(0x80000000)
    return lax.bitcast_convert_type(bits, F32)


def _dot_nt(a, b):
    return lax.dot_general(a, b, (((1,), (1,)), ((), ())), preferred_element_type=F32)


def _dot_tn(a, b):
    return lax.dot_general(a, b, (((0,), (0,)), ((), ())), preferred_element_type=F32)


def _gla_kernel(*refs, n, with_out):
    if with_out:
        (q_ref, ff_ref, fb_ref, i_ref, lb_ref, s0f_ref, s0b_ref,
         o_ref, sf_ref, sb_ref, qs, vs, cum_f, cum_b, k_f, k_b) = refs
    else:
        (ff_ref, fb_ref, i_ref, lb_ref, s0f_ref, s0b_ref,
         sf_ref, sb_ref, vs, cum_f, cum_b, k_f, k_b) = refs
    c = CHUNK
    nchunks = n // c
    nblk = c // DIAG

    if with_out:
        qs[...] = _silu(q_ref[0].astype(F32)) * (HEAD_DIM ** -0.5)
    vs[...] = i_ref[0].astype(F32)

    pos_n = lax.broadcasted_iota(jnp.int32, (n, HEAD_DIM), 0) % c
    for rev in (False, True):
        z = (fb_ref if rev else ff_ref)[0].astype(F32)
        lb = lb_ref[1:2, :] if rev else lb_ref[0:1, :]
        e = jnp.exp(-jnp.abs(z))
        log_sig = jnp.minimum(z, 0.0) - jnp.log(1.0 + e)
        a = jnp.log(lb)
        bt = jnp.log(1.0 - lb) + log_sig
        mx = jnp.maximum(a, bt)
        mn = jnp.minimum(a, bt)
        s = mx + jnp.log(1.0 + jnp.exp(mn - mx))
        (k_b if rev else k_f)[...] = (1.0 - lb) * (jnp.where(z >= 0.0, e, 1.0) / (1.0 + e))
        sh = 1
        while sh < c:
            if rev:
                s = s + jnp.where(pos_n < c - sh, pltpu.roll(s, n - sh, 0), 0.0)
            else:
                s = s + jnp.where(pos_n >= sh, pltpu.roll(s, sh, 0), 0.0)
            sh *= 2
        (cum_b if rev else cum_f)[...] = s

    row = lax.broadcasted_iota(jnp.int32, (c, c), 0)
    col = lax.broadcasted_iota(jnp.int32, (c, c), 1)
    rowv = lax.broadcasted_iota(jnp.int32, (c, HEAD_DIM), 0)
    row3 = lax.broadcasted_iota(jnp.int32, (nblk, DIAG, HEAD_DIM), 1)

    def level_map(rev):
        lv = jnp.zeros((c, c), jnp.int32)
        m = c
        while m > 1:
            half = m // 2
            t_right = (row % m) >= half
            s_right = (col % m) >= half
            pair = jnp.logical_and(jnp.logical_not(t_right), s_right) if rev else \
                jnp.logical_and(t_right, jnp.logical_not(s_right))
            lv = jnp.where(jnp.logical_and((row // m) == (col // m), pair), m, lv)
            m = half
        return lv

    lv_f, lv_b = level_map(False), level_map(True)
    t_rows_of = {}
    m = c
    while m > 1:
        in_right = (rowv % m) >= m // 2
        t_rows_of[(False, m)] = in_right
        t_rows_of[(True, m)] = jnp.logical_not(in_right)
        m //= 2

    def one_chunk(rev, cidx, st, accumulate):
        r0 = pl.multiple_of(cidx * c, c)
        cum = (cum_b if rev else cum_f)[pl.ds(r0, c), :]
        kc = (k_b if rev else k_f)[pl.ds(r0, c), :]
        vc = vs[pl.ds(r0, c), :]
        vcb = vc.astype(BF16)
        tot = cum[0:1, :] if rev else cum[c - 1:c, :]

        if with_out:
            qc = qs[pl.ds(r0, c), :]
            o = _dot_nt((qc * jnp.exp(cum)).astype(BF16), st.astype(BF16))
            lv = lv_b if rev else lv_f
            c3 = cum.reshape(nblk, DIAG, HEAD_DIM)
            att = jnp.zeros((c, c), F32)
            m = c
            while m > 1:
                half = m // 2
                t_rows = t_rows_of[(rev, m)]
                if m > DIAG:
                    pieces = []
                    for blk in range(c // m):
                        rr = blk * m + (half if rev else half - 1)
                        pieces.append(jnp.broadcast_to(cum[rr:rr + 1, :], (m, HEAD_DIM)))
                    ref = pieces[0] if len(pieces) == 1 else jnp.concatenate(pieces, axis=0)
                elif m > 2:
                    ref3 = None
                    for grp in reversed(range(DIAG // m)):
                        rr = grp * m + (half if rev else half - 1)
                        piece = jnp.broadcast_to(c3[:, rr:rr + 1, :], (nblk, DIAG, HEAD_DIM))
                        ref3 = piece if ref3 is None else jnp.where(row3 < (grp + 1) * m, piece, ref3)
                    ref = ref3.reshape(c, HEAD_DIM)
                else:
                    ref = jnp.where(t_rows, pltpu.roll(cum, (c - 1) if rev else 1, 0), cum)
                w = jnp.where(t_rows, qc, kc) * jnp.exp(_neg_abs(cum - ref))
                wb = w.astype(BF16)
                att = jnp.where(lv == m, _dot_nt(wb, wb), att)
                m = half
            o = o + jnp.dot(att.astype(BF16), vcb, preferred_element_type=F32)
            o = o + jnp.sum(qc * kc, axis=-1, keepdims=True) * vc
            if accumulate:
                o_ref[0, pl.ds(r0, c), :] = o_ref[0, pl.ds(r0, c), :] + o
            else:
                o_ref[0, pl.ds(r0, c), :] = o

        khat = kc * jnp.exp(tot - cum)
        return jnp.exp(tot) * st + _dot_tn(vcb, khat.astype(BF16))

    def make_body(accumulate):
        def body(ci, carry):
            st_f, st_b = carry
            st_f = one_chunk(False, ci, st_f, accumulate)
            st_b = one_chunk(True, nchunks - 1 - ci, st_b, accumulate)
            return st_f, st_b
        return body

    carry = (s0f_ref[0, 0], s0b_ref[0, 0])
    carry = lax.fori_loop(0, nchunks // 2, make_body(False), carry)
    carry = lax.fori_loop(nchunks // 2, nchunks, make_body(True), carry)
    sf_ref[0, 0] = carry[0]
    sb_ref[0, 0] = carry[1]


def _gla(p, lb, s0f, s0b, *, with_out, col0):
    b, n, _ = p.shape
    assert n % (2 * CHUNK) == 0
    hb = D_HGRN // HEAD_DIM

    def col_spec(k):
        return pl.BlockSpec((1, n, HEAD_DIM), lambda bi, h, k=k: (bi, 0, col0 + k * hb + h))

    st_spec = pl.BlockSpec((1, 1, HEAD_DIM, HEAD_DIM), lambda bi, h: (bi, h, 0, 0))
    n_cols = 4 if with_out else 3
    in_specs = [col_spec(k) for k in range(n_cols)] + [
        pl.BlockSpec((2, HEAD_DIM), lambda bi, h: (0, h)), st_spec, st_spec]
    st_shape = jax.ShapeDtypeStruct((b, HEADS, HEAD_DIM, HEAD_DIM), F32)
    out_specs = [st_spec, st_spec]
    out_shape = [st_shape, st_shape]
    scratch = [pltpu.VMEM((n, HEAD_DIM), F32) for _ in range(5)]
    if with_out:
        out_specs = [pl.BlockSpec((1, n, HEAD_DIM), lambda bi, h: (bi, 0, h))] + out_specs
        out_shape = [jax.ShapeDtypeStruct((b, n, D_HGRN), F32)] + out_shape
        scratch = [pltpu.VMEM((n, HEAD_DIM), F32)] + scratch
    return pl.pallas_call(
        functools.partial(_gla_kernel, n=n, with_out=with_out),
        grid=(b, HEADS),
        in_specs=in_specs,
        out_specs=out_specs,
        out_shape=out_shape,
        scratch_shapes=scratch,
        compiler_params=_cparams(("parallel", "parallel")),
        name="gla" if with_out else "gla_state",
    )(*([p] * n_cols), lb, s0f, s0b)


def _pool_kernel(u_ref, pw_ref, ps_ref, o_ref, *, n, width):
    rows = n // width
    t = lax.broadcasted_iota(jnp.int32, (n, POOL_GROUP), 0)
    colp = t % width
    rowp = t // width

    def shift(a, delta, pos, length, stride):
        rolled = pltpu.roll(a, (delta * stride) % n, 0)
        ok = jnp.logical_and(pos - delta >= 0, pos - delta < length)
        return jnp.where(ok, rolled, 0.0)

    def box_sum(a, w, pos, length, stride):
        trail, lead = a, a
        h = 1
        while h < w // 2:
            trail = trail + shift(trail, h, pos, length, stride)
            lead = lead + shift(lead, -h, pos, length, stride)
            h *= 2
        return shift(trail, 1, pos, length, stride) + lead

    def count(pos, w, length):
        lo = jnp.clip(pos - w // 2, 0, length)
        hi = jnp.clip(pos - w // 2 + w, 0, length)
        return (hi - lo).astype(F32)

    for gi, w in enumerate(POOL_WINDOWS):
        ug = u_ref[0, :, gi * POOL_GROUP:(gi + 1) * POOL_GROUP].astype(F32)
        if rows > 1:
            s = box_sum(ug, w, rowp, rows, width) / count(rowp, w, rows)
            s = box_sum(s, w, colp, width, 1) / count(colp, w, width)
        else:
            s = box_sum(ug, w, colp, width, 1) / count(colp, w, width)
        y = jnp.dot((s - ug).astype(BF16), pw_ref[gi], preferred_element_type=F32)
        y = y * ps_ref[:, gi * POOL_GROUP:(gi + 1) * POOL_GROUP]
        o_ref[0, :, gi * POOL_GROUP:(gi + 1) * POOL_GROUP] = y.astype(o_ref.dtype)


def _pool(p, pool_w, pool_scale, *, on_grid, col_block):
    b, n, _ = p.shape
    width = GRID_W if on_grid else n
    assert n % width == 0 and width & (width - 1) == 0
    return pl.pallas_call(
        functools.partial(_pool_kernel, n=n, width=width),
        grid=(b,),
        in_specs=[
            pl.BlockSpec((1, n, D_POOL), lambda bi: (bi, 0, col_block)),
            pl.BlockSpec(pool_w.shape, lambda bi: (0, 0, 0)),
            pl.BlockSpec((1, D_POOL), lambda bi: (0, 0)),
        ],
        out_specs=pl.BlockSpec((1, n, D_POOL), lambda bi: (bi, 0, 0)),
        out_shape=jax.ShapeDtypeStruct((b, n, D_POOL), BF16),
        compiler_params=_cparams(("parallel",)),
        name="pool",
    )(p, pool_w, pool_scale.reshape(1, D_POOL))


def _merge_kernel(*refs, moe, n_experts):
    (o_ref, g_ref, yp_ref, ga_ref, gb_ref, x_ref, mod_ref, og_ref, post_ref, pre2_ref,
     wr_ref, wp_ref, wo_ref) = refs[:13]
    if moe:
        router_ref, xn_ref, hrow_ref, route_ref = refs[13:]
    else:
        xn_ref, h2_ref = refs[13:]
    tm = x_ref.shape[1]
    m = mod_ref[0]
    o = o_ref[0]
    parts = []
    for h in range(HEADS):
        oh = o[:, h * HEAD_DIM:(h + 1) * HEAD_DIM]
        parts.append(oh * lax.rsqrt(jnp.mean(oh * oh, axis=-1, keepdims=True) + EPS))
    on = jnp.concatenate(parts, axis=-1) * og_ref[...]
    y_rec = (on * _silu(g_ref[0].astype(F32))).astype(BF16)
    rec = jnp.dot(y_rec, wr_ref[...], preferred_element_type=F32)
    pool = jnp.dot(yp_ref[0], wp_ref[...], preferred_element_type=F32)
    merged = _sigmoid(ga_ref[0].astype(F32)) * rec + _sigmoid(gb_ref[0].astype(F32)) * pool
    y = jnp.dot(merged.astype(BF16), wo_ref[...], preferred_element_type=F32)
    xn = x_ref[0] + m[2:3] * _rms(y, post_ref[...])
    xn_ref[0] = xn
    h2 = _rms(xn, pre2_ref[...]) * (1.0 + m[4:5]) + m[3:4]
    if not moe:
        h2_ref[0] = h2.astype(h2_ref.dtype)
        return
    d = h2.shape[-1]
    for s in range(d // LANES):
        hrow_ref[pl.ds(s, tm, stride=d // LANES), :] = h2[:, s * LANES:(s + 1) * LANES]
    h_hi = h2.astype(BF16)
    h_lo = (h2 - h_hi.astype(F32)).astype(BF16)
    l2 = jnp.dot(h_hi, router_ref[...], preferred_element_type=F32)
    logits = (l2[:, :LANES] + l2[:, LANES:]
              + jnp.dot(h_lo, router_ref[:, :LANES], preferred_element_type=F32))
    lane = lax.broadcasted_iota(jnp.int32, logits.shape, 1)
    neg = jnp.float32(-jnp.inf)
    lg = jnp.where(lane < n_experts, logits, neg)
    m1 = jnp.max(lg, axis=-1, keepdims=True)
    i1 = jnp.min(jnp.where(lg == m1, lane, LANES), axis=-1, keepdims=True)
    lg2 = jnp.where(lane == i1, neg, lg)
    m2 = jnp.max(lg2, axis=-1, keepdims=True)
    i2 = jnp.min(jnp.where(lg2 == m2, lane, LANES), axis=-1, keepdims=True)
    e2 = jnp.exp(m2 - m1)
    p1 = 1.0 / (1.0 + e2)
    p2 = e2 * p1
    route = jnp.where(lane == 0, i1.astype(F32),
                      jnp.where(lane == 1, i2.astype(F32),
                                jnp.where(lane == 2, p1, jnp.where(lane == 3, p2, 0.0))))
    route_ref[...] = route


def _merge(o, p, ypool, xs, mod_l, mod_row, o_gain, post_gain, pre2_gain, w_rec, w_pool, w_out,
           router=None):
    b, n, d = xs.shape
    moe = router is not None
    tm = _pick_tile(n, 512, SUBLANES)
    nt = n // tm
    if mod_row is None:
        mod_map = lambda bi, i: (bi, 0, 0)
    else:
        mod_map = lambda bi, i: (mod_row, 0, 0)
    g_blk = 4 * D_HGRN // D_HGRN
    ga_col = 5 * D_HGRN + D_POOL
    assert ga_col % d == 0
    ga_blk = ga_col // d
    const = lambda bi, i: (0, 0)
    in_specs = [
        pl.BlockSpec((1, tm, D_HGRN), lambda bi, i: (bi, i, 0)),
        pl.BlockSpec((1, tm, D_HGRN), lambda bi, i: (bi, i, g_blk)),
        pl.BlockSpec((1, tm, D_POOL), lambda bi, i: (bi, i, 0)),
        pl.BlockSpec((1, tm, d), lambda bi, i: (bi, i, ga_blk)),
        pl.BlockSpec((1, tm, d), lambda bi, i: (bi, i, ga_blk + 1)),
        pl.BlockSpec((1, tm, d), lambda bi, i: (bi, i, 0)),
        pl.BlockSpec((1, 6, d), mod_map),
        pl.BlockSpec((1, D_HGRN), const),
        pl.BlockSpec((1, d), const),
        pl.BlockSpec((1, d), const),
        pl.BlockSpec((D_HGRN, d), const),
        pl.BlockSpec((D_POOL, d), const),
        pl.BlockSpec((d, d), const),
    ]
    args = [o, p, ypool, p, p, xs, mod_l, o_gain.reshape(1, D_HGRN), post_gain.reshape(1, d),
            pre2_gain.reshape(1, d), w_rec, w_pool, w_out]
    out_specs = [pl.BlockSpec((1, tm, d), lambda bi, i: (bi, i, 0))]
    out_shape = [jax.ShapeDtypeStruct((b, n, d), F32)]
    n_experts = 0
    if moe:
        n_experts = router.shape[1]
        router_pad = jnp.zeros((d, LANES), F32).at[:, :n_experts].set(router)
        r_hi = router_pad.astype(BF16)
        r_lo = (router_pad - r_hi.astype(F32)).astype(BF16)
        in_specs.append(pl.BlockSpec((d, 2 * LANES), const))
        args.append(jnp.concatenate([r_hi, r_lo], axis=1))
        sub = d // LANES
        out_specs += [pl.BlockSpec((tm * sub, LANES), lambda bi, i: (bi * nt + i, 0)),
                      pl.BlockSpec((tm, LANES), lambda bi, i: (bi * nt + i, 0))]
        out_shape += [jax.ShapeDtypeStruct((b * n * sub, LANES), F32),
                      jax.ShapeDtypeStruct((b * n, LANES), F32)]
    else:
        out_specs.append(pl.BlockSpec((1, tm, d), lambda bi, i: (bi, i, 0)))
        out_shape.append(jax.ShapeDtypeStruct((b, n, d), BF16))
    return pl.pallas_call(
        functools.partial(_merge_kernel, moe=moe, n_experts=n_experts),
        grid=(b, nt),
        in_specs=in_specs,
        out_specs=out_specs,
        out_shape=out_shape,
        compiler_params=_cparams(("parallel", "parallel")),
        name="merge_moe" if moe else "merge",
    )(*args)


def _ffn_kernel(h_ref, wg_ref, wu_ref, wd_ref, x_ref, mod_ref, post_ref, o_ref, acc):
    j = pl.program_id(1)

    @pl.when(j == 0)
    def _():
        acc[...] = jnp.zeros_like(acc)

    h = h_ref[...]
    g = jnp.dot(h, wg_ref[...], preferred_element_type=F32)
    u = jnp.dot(h, wu_ref[...], preferred_element_type=F32)
    acc[...] += jnp.dot((_silu(g) * u).astype(BF16), wd_ref[...], preferred_element_type=F32)

    @pl.when(j == pl.num_programs(1) - 1)
    def _():
        m = mod_ref[0]
        o_ref[...] = x_ref[...] + m[5:6] * _rms(acc[...], post_ref[...])


def _ffn_dense(h2, xs, mod_l, mod_row, post_gain, wg, wu, wd):
    b, n, d = xs.shape
    f = wg.shape[1]
    r = b * n
    tm = _pick_tile(n, 512, SUBLANES)
    tf = _pick_tile(f, 1408, LANES)
    tiles_per_batch = n // tm
    if mod_row is None:
        mod_map = lambda i, j: (i // tiles_per_batch, 0, 0)
    else:
        mod_map = lambda i, j: (mod_row, 0, 0)
    out = pl.pallas_call(
        _ffn_kernel,
        grid=(r // tm, f // tf),
        in_specs=[
            pl.BlockSpec((tm, d), lambda i, j: (i, 0)),
            pl.BlockSpec((d, tf), lambda i, j: (0, j)),
            pl.BlockSpec((d, tf), lambda i, j: (0, j)),
            pl.BlockSpec((tf, d), lambda i, j: (j, 0)),
            pl.BlockSpec((tm, d), lambda i, j: (i, 0)),
            pl.BlockSpec((1, 6, d), mod_map),
            pl.BlockSpec((1, d), lambda i, j: (0, 0)),
        ],
        out_specs=pl.BlockSpec((tm, d), lambda i, j: (i, 0)),
        out_shape=jax.ShapeDtypeStruct((r, d), F32),
        scratch_shapes=[pltpu.VMEM((tm, d), F32)],
        compiler_params=_cparams(("parallel", "arbitrary")),
        name="ffn_dense",
    )(h2.reshape(r, d), wg, wu, wd, xs.reshape(r, d), mod_l, post_gain.reshape(1, d))
    return out.reshape(b, n, d)


def _moe_kernel(te_ref, tv_ref, idx_ref, idx_next_ref, h_hbm, wg_ref, wu_ref, wd_ref, o_ref,
                xg, xb, acc, sem, *, tm, sub, rps, nf):
    i = pl.program_id(0)
    j = pl.program_id(1)
    slot = i % 2
    rows_buf = rps * nf

    def row_copy(ids_ref, dst_slot, r):
        tok = ids_ref[0, 0, jnp.minimum(r, tm - 1)]
        src = h_hbm.at[pl.ds(pl.multiple_of(tok * sub, sub), sub), :]
        dst = xg.at[dst_slot, pl.ds(pl.multiple_of(r * sub, sub), sub), :]
        return pltpu.make_async_copy(src, dst, sem.at[dst_slot])

    @pl.when(j == 0)
    def _():
        @pl.when(i == 0)
        def _():
            def issue(r, carry):
                row_copy(idx_ref, 0, r).start()
                return carry
            lax.fori_loop(0, rows_buf, issue, 0)

        @pl.when(jnp.logical_or(i == 0, tv_ref[jnp.maximum(i - 1, 0)] > 0))
        def _():
            pltpu.make_async_copy(h_hbm.at[pl.ds(0, rows_buf * sub), :], xg.at[slot],
                                  sem.at[slot]).wait()

        for s in range(sub):
            xb[:, s * LANES:(s + 1) * LANES] = xg[slot, pl.ds(s, tm, stride=sub), :].astype(BF16)
        acc[...] = jnp.zeros_like(acc)

    @pl.when(tv_ref[i] > 0)
    def _():
        base = j * rps
        for k in range(rps):
            row_copy(idx_next_ref, 1 - slot, base + k).start()
        h = xb[...]
        g = jnp.dot(h, wg_ref[0], preferred_element_type=F32)
        u = jnp.dot(h, wu_ref[0], preferred_element_type=F32)
        acc[...] += jnp.dot((_silu(g) * u).astype(BF16), wd_ref[0], preferred_element_type=F32)

    @pl.when(j == nf - 1)
    def _():
        for s in range(sub):
            o_ref[pl.ds(s, tm, stride=sub), :] = acc[:, s * LANES:(s + 1) * LANES]


COMBINE_UNROLL = 8


def _combine_kernel(pos_ref, pos_next_ref, y_hbm, route_ref, x_ref, mod_ref, post_ref, o_ref,
                    yb, sem, *, tm, sub):
    i = pl.program_id(0)
    slot = i % 2
    n_rows = TOP_K * tm

    def start_all(p_ref, dst_slot):
        def group(gi, carry):
            for k in range(COMBINE_UNROLL):
                a = gi * COMBINE_UNROLL + k
                src = y_hbm.at[pl.ds(pl.multiple_of(p_ref[0, 0, a] * sub, sub), sub), :]
                dst = yb.at[dst_slot, pl.ds(pl.multiple_of(a * sub, sub), sub), :]
                pltpu.make_async_copy(src, dst, sem.at[dst_slot]).start()
            return carry
        lax.fori_loop(0, n_rows // COMBINE_UNROLL, group, 0)

    @pl.when(i == 0)
    def _():
        start_all(pos_ref, 0)

    @pl.when(i + 1 < pl.num_programs(0))
    def _():
        start_all(pos_next_ref, 1 - slot)

    pltpu.make_async_copy(y_hbm.at[pl.ds(0, n_rows * sub), :], yb.at[slot], sem.at[slot]).wait()

    rt = route_ref[...]
    y1 = jnp.concatenate([yb[slot, pl.ds(s, tm, stride=sub), :] for s in range(sub)], axis=-1)
    y2 = jnp.concatenate([yb[slot, pl.ds(tm * sub + s, tm, stride=sub), :] for s in range(sub)],
                         axis=-1)
    y = rt[:, 2:3] * y1 + rt[:, 3:4] * y2
    m = mod_ref[0]
    o_ref[...] = x_ref[...] + m[5:6] * _rms(y, post_ref[...])


def _ffn_moe(hrows, route, xs, mod_l, mod_row, post_gain, wg, wu, wd):
    b, n, d = xs.shape
    r = b * n
    n_exp, _, f = wg.shape
    sub = d // LANES
    tm = _pick_tile(r, 1024, SUBLANES)
    tf = _pick_tile(f, 512, LANES)
    n_tiles = TOP_K * r // tm + n_exp
    assert (TOP_K * r) % tm == 0

    e_flat = route[:, :TOP_K].astype(jnp.int32).reshape(-1)
    onehot = (e_flat[:, None] == jnp.arange(n_exp, dtype=jnp.int32)[None, :]).astype(jnp.int32)
    csum = jnp.cumsum(onehot, axis=0)
    rank = jnp.sum((csum - onehot) * onehot, axis=1)
    counts = csum[-1]
    tiles_e = (counts + tm - 1) // tm
    tile_end = jnp.cumsum(tiles_e)
    tile_start = tile_end - tiles_e
    pos = tile_start[e_flat] * tm + rank
    src = jnp.zeros((n_tiles * tm,), jnp.int32).at[pos].set(
        jnp.arange(TOP_K * r, dtype=jnp.int32) // TOP_K, unique_indices=True)
    tile_ids = jnp.arange(n_tiles, dtype=jnp.int32)
    tile_valid = (tile_ids < tile_end[-1]).astype(jnp.int32)
    tile_expert = jnp.minimum(
        jnp.sum((tile_ids[:, None] >= tile_end[None, :]).astype(jnp.int32), axis=1), n_exp - 1)
    src3 = src.reshape(n_tiles, 1, tm)

    nf = f // tf
    rps = -(-tm // nf)
    ysorted = pl.pallas_call(
        functools.partial(_moe_kernel, tm=tm, sub=sub, rps=rps, nf=nf),
        grid_spec=pltpu.PrefetchScalarGridSpec(
            num_scalar_prefetch=2,
            grid=(n_tiles, nf),
            in_specs=[
                pl.BlockSpec((1, 1, tm), lambda i, j, te, tv: (i, 0, 0),
                             memory_space=pltpu.MemorySpace.SMEM),
                pl.BlockSpec((1, 1, tm), lambda i, j, te, tv: (jnp.minimum(i + 1, n_tiles - 1), 0, 0),
                             memory_space=pltpu.MemorySpace.SMEM),
                pl.BlockSpec(memory_space=pl.ANY),
                pl.BlockSpec((1, d, tf), lambda i, j, te, tv: (te[i], 0, j)),
                pl.BlockSpec((1, d, tf), lambda i, j, te, tv: (te[i], 0, j)),
                pl.BlockSpec((1, tf, d), lambda i, j, te, tv: (te[i], j, 0)),
            ],
            out_specs=pl.BlockSpec((tm * sub, LANES), lambda i, j, te, tv: (i, 0)),
            scratch_shapes=[
                pltpu.VMEM((2, rps * nf * sub, LANES), F32),
                pltpu.VMEM((tm, d), BF16),
                pltpu.VMEM((tm, d), F32),
                pltpu.SemaphoreType.DMA((2,)),
            ],
        ),
        out_shape=jax.ShapeDtypeStruct((n_tiles * tm * sub, LANES), F32),
        compiler_params=_cparams(("arbitrary", "arbitrary")),
        name="moe_experts",
    )(tile_expert, tile_valid, src3, src3, hrows, wg, wu, wd)

    tmc = _pick_tile(n, 512, SUBLANES)
    tiles_per_batch = n // tmc
    if mod_row is None:
        mod_map = lambda i: (i // tiles_per_batch, 0, 0)
    else:
        mod_map = lambda i: (mod_row, 0, 0)
    nct = r // tmc
    assert (TOP_K * tmc) % COMBINE_UNROLL == 0
    pos3 = pos.reshape(nct, tmc, TOP_K).transpose(0, 2, 1).reshape(nct, 1, TOP_K * tmc)
    out = pl.pallas_call(
        functools.partial(_combine_kernel, tm=tmc, sub=sub),
        grid=(r // tmc,),
        in_specs=[
            pl.BlockSpec((1, 1, TOP_K * tmc), lambda i: (i, 0, 0),
                         memory_space=pltpu.MemorySpace.SMEM),
            pl.BlockSpec((1, 1, TOP_K * tmc), lambda i: (jnp.minimum(i + 1, nct - 1), 0, 0),
                         memory_space=pltpu.MemorySpace.SMEM),
            pl.BlockSpec(memory_space=pl.ANY),
            pl.BlockSpec((tmc, LANES), lambda i: (i, 0)),
            pl.BlockSpec((tmc, d), lambda i: (i, 0)),
            pl.BlockSpec((1, 6, d), mod_map),
            pl.BlockSpec((1, d), lambda i: (0, 0)),
        ],
        out_specs=pl.BlockSpec((tmc, d), lambda i: (i, 0)),
        out_shape=jax.ShapeDtypeStruct((r, d), F32),
        scratch_shapes=[
            pltpu.VMEM((2, TOP_K * tmc * sub, LANES), F32),
            pltpu.SemaphoreType.DMA((2,)),
        ],
        compiler_params=_cparams(("arbitrary",)),
        name="moe_combine",
    )(pos3, pos3, ysorted, route, xs.reshape(r, d), mod_l, post_gain.reshape(1, d))
    return out.reshape(b, n, d)


def kernel(x, c, ctx, c_ctx, ada_w, ada_b, mix_pre, mix_post, ffn_pre, ffn_post, w_in, hgrn_lb, hgrn_gain,
           pool_w, pool_scale, w_branch_rec, w_branch_pool, w_out, ffn_w_gate, ffn_w_up, ffn_w_down,
           moe_router, moe_w_gate, moe_w_up, moe_w_down):
    b, n, d = x.shape
    depth = ada_w.shape[0]
    assert b + 1 <= MOD_ROWS and d % LANES == 0
    ctx_row = b

    cond = jnp.zeros((MOD_ROWS, d), F32).at[:b].set(c).at[b].set(c_ctx)
    mod = _modulation(cond, ada_w, ada_b).reshape(depth, MOD_ROWS, 6, d)

    lb_all = jnp.cumsum(jax.nn.softmax(hgrn_lb.astype(F32), axis=0), axis=0)
    lb_all = lb_all - lb_all[0:1]

    bf = lambda a: a.astype(BF16)
    w_in_b, pool_w_b = bf(w_in), bf(pool_w)
    w_rec_b, w_pool_b, w_out_b = bf(w_branch_rec), bf(w_branch_pool), bf(w_out)
    ffn_g, ffn_u, ffn_d = bf(ffn_w_gate), bf(ffn_w_up), bf(ffn_w_down)
    moe_g, moe_u, moe_d = bf(moe_w_gate), bf(moe_w_up), bf(moe_w_down)

    u_blk = (5 * D_HGRN) // D_POOL
    s_zero = jnp.zeros((b, HEADS, HEAD_DIM, HEAD_DIM), F32)

    def mixer(xs, l, mod_row, s0f, s0b, on_grid, router):
        p = _in_proj(xs, mod[l], mod_row, mix_pre[l], w_in_b[l])
        o, s_f, s_b = _gla(p, lb_all[l], s0f, s0b, with_out=True, col0=0)
        yp = _pool(p, pool_w_b[l], pool_scale[l], on_grid=on_grid, col_block=u_blk)
        outs = _merge(o, p, yp, xs, mod[l], mod_row, hgrn_gain[l], mix_post[l], ffn_pre[l],
                      w_rec_b[l], w_pool_b[l], w_out_b[l], router=router)
        return outs, s_f, s_b

    def channel(outs, l, mod_row):
        j = l // 2
        if l % 2 == 0:
            xn, h2 = outs
            return _ffn_dense(h2, xn, mod[l], mod_row, ffn_post[l], ffn_g[j], ffn_u[j], ffn_d[j])
        xn, hrows, route = outs
        return _ffn_moe(hrows, route, xn, mod[l], mod_row, ffn_post[l], moe_g[j], moe_u[j], moe_d[j])

    cx = ctx
    for l in range(depth):
        last = l == depth - 1
        router = moe_router[l // 2] if l % 2 == 1 else None
        if last:
            pc = _in_proj(cx, mod[l], ctx_row, mix_pre[l], w_in_b[l][:, D_HGRN:4 * D_HGRN])
            s_f, s_b = _gla(pc, lb_all[l], s_zero, s_zero, with_out=False, col0=0)
            c_outs = None
        else:
            c_outs, s_f, s_b = mixer(cx, l, ctx_row, s_zero, s_zero, False, router)
        x_outs, _, _ = mixer(x, l, None, s_f, s_b, True, router)
        x = channel(x_outs, l, None)
        if not last:
            cx = channel(c_outs, l, ctx_row)
    return x
```

```python
import functools

import jax
import jax.numpy as jnp
from jax import lax
from jax.experimental import pallas as pl
from jax.experimental.pallas import tpu as pltpu

F32 = jnp.float32
BF16 = jnp.bfloat16

EPS = 1e-6
LANES = 128
SUBLANES = 8
HEADS = 4
HEAD_DIM = 128
D_HGRN = HEADS * HEAD_DIM
D_POOL = 512
POOL_WINDOWS = (2, 4, 8, 16)
POOL_GROUP = D_POOL // len(POOL_WINDOWS)
GRID_W = 64
CHUNK = 64
SUBCHUNK = CHUNK // 2
SAFE_DECAY = 80.0
DIAG = SUBLANES
GLA_HEADS_PER_STEP = 2
TOP_K = 2
MOD_ROWS = 40
VMEM_LIMIT = 56 * 1024 * 1024

def _cparams(sem):
    return pltpu.CompilerParams(dimension_semantics=sem, vmem_limit_bytes=VMEM_LIMIT)


def _pick_tile(n, cap, mult):
    if n <= cap:
        return n
    best = None
    for t in range(mult, cap + 1, mult):
        if n % t == 0:
            best = t
    assert best is not None, (n, cap, mult)
    return best


def _sigmoid(z):
    return 1.0 / (1.0 + jnp.exp(-z))


def _silu(z):
    return z * _sigmoid(z)


def _mod_kernel(c_ref, w_ref, b_ref, o_ref):
    a = _silu(c_ref[...])
    o_ref[0] = jnp.dot(a, w_ref[0], preferred_element_type=F32,
                       precision=lax.Precision.HIGHEST) + b_ref[0]


def _modulation(cond, ada_w, ada_b):
    depth, d, six_d = ada_w.shape
    tn = _pick_tile(six_d, 1024, LANES)
    return pl.pallas_call(
        _mod_kernel,
        grid=(depth, six_d // tn),
        in_specs=[
            pl.BlockSpec((MOD_ROWS, d), lambda l, j: (0, 0)),
            pl.BlockSpec((1, d, tn), lambda l, j: (l, 0, j)),
            pl.BlockSpec((1, 1, tn), lambda l, j: (l, 0, j)),
        ],
        out_specs=pl.BlockSpec((1, MOD_ROWS, tn), lambda l, j: (l, 0, j)),
        out_shape=jax.ShapeDtypeStruct((depth, MOD_ROWS, six_d), F32),
        compiler_params=_cparams(("parallel", "parallel")),
        name="modulation",
    )(cond, ada_w, ada_b.reshape(depth, 1, six_d))


def _rms(x, gain):
    ms = jnp.mean(x * x, axis=-1, keepdims=True)
    return x * lax.rsqrt(ms + EPS) * gain


def _inproj_kernel(x_ref, mod_ref, gain_ref, w_ref, o_ref, *, tn):
    m = mod_ref[0]
    h = _rms(x_ref[0], gain_ref[...]) * (1.0 + m[1:2]) + m[0:1]
    hb = h.astype(BF16)
    for j in range(w_ref.shape[1] // tn):
        o_ref[0, :, j * tn:(j + 1) * tn] = jnp.dot(
            hb, w_ref[:, j * tn:(j + 1) * tn], preferred_element_type=F32).astype(o_ref.dtype)


def _in_proj(xs, mod_l, mod_row, gain, w):
    b, n, d = xs.shape
    nc = w.shape[1]
    tm = _pick_tile(n, 512, SUBLANES)
    tn = _pick_tile(nc, 512, LANES)
    if mod_row is None:
        mod_map = lambda bi, i: (bi, 0, 0)
    else:
        mod_map = lambda bi, i: (mod_row, 0, 0)
    return pl.pallas_call(
        functools.partial(_inproj_kernel, tn=tn),
        grid=(b, n // tm),
        in_specs=[
            pl.BlockSpec((1, tm, d), lambda bi, i: (bi, i, 0)),
            pl.BlockSpec((1, 6, d), mod_map),
            pl.BlockSpec((1, d), lambda bi, i: (0, 0)),
            pl.BlockSpec((d, nc), lambda bi, i: (0, 0)),
        ],
        out_specs=pl.BlockSpec((1, tm, nc), lambda bi, i: (bi, i, 0)),
        out_shape=jax.ShapeDtypeStruct((b, n, nc), BF16),
        compiler_params=_cparams(("parallel", "parallel")),
        name="in_proj",
    )(xs, mod_l, gain.reshape(1, d), w)


def _neg_abs(a):
    bits = lax.bitcast_convert_type(a, jnp.uint32) | jnp.uint32(0x80000000)
    return lax.bitcast_convert_type(bits, F32)


def _dot_nt(a, b):
    return lax.dot_general(a, b, (((1,), (1,)), ((), ())), preferred_element_type=F32)


def _dot_tn(a, b):
    return lax.dot_general(a, b, (((0,), (0,)), ((), ())), preferred_element_type=F32)


def _gla_kernel(*refs, n, with_out, hpb):
    if with_out:
        (q_ref, ff_ref, fb_ref, i_ref, lb_ref, s0f_ref, s0b_ref,
         o_ref, sf_ref, sb_ref, qs, vs, cum_f, cum_b, k_f, k_b) = refs
    else:
        (ff_ref, fb_ref, i_ref, lb_ref, s0f_ref, s0b_ref,
         sf_ref, sb_ref, vs, cum_f, cum_b, k_f, k_b) = refs
    c = CHUNK
    nchunks = n // c
    nblk = c // DIAG

    if with_out:
        qs[...] = _silu(q_ref[0].astype(F32)) * (HEAD_DIM ** -0.5)
    vs[...] = i_ref[0].astype(F32)

    hb = SUBCHUNK
    pos_n = lax.broadcasted_iota(jnp.int32, (n, hpb * HEAD_DIM), 0) % hb
    lowest = None
    for rev in (False, True):
        z = (fb_ref if rev else ff_ref)[0].astype(F32)
        lb = lb_ref[1:2, :] if rev else lb_ref[0:1, :]
        e = jnp.exp(-jnp.abs(z))
        log_sig = jnp.minimum(z, 0.0) - jnp.log(1.0 + e)
        a = jnp.log(lb)
        bt = jnp.log(1.0 - lb) + log_sig
        mx = jnp.maximum(a, bt)
        mn = jnp.minimum(a, bt)
        s = mx + jnp.log(1.0 + jnp.exp(mn - mx))
        (k_b if rev else k_f)[...] = (1.0 - lb) * (jnp.where(z >= 0.0, e, 1.0) / (1.0 + e))
        sh = 1
        while sh < hb:
            if rev:
                s = s + jnp.where(pos_n < hb - sh, pltpu.roll(s, n - sh, 0), 0.0)
            else:
                s = s + jnp.where(pos_n >= sh, pltpu.roll(s, sh, 0), 0.0)
            sh *= 2
        sums_ref = cum_b if rev else cum_f
        sums_ref[...] = s

        def running_min(i, acc):
            return jnp.minimum(acc, sums_ref[pl.ds(pl.multiple_of(i * c, c), c), :])

        low = jnp.min(lax.fori_loop(1, nchunks, running_min, sums_ref[pl.ds(0, c), :]))
        lowest = low if lowest is None else jnp.minimum(lowest, low)

    row = lax.broadcasted_iota(jnp.int32, (c, c), 0)
    col = lax.broadcasted_iota(jnp.int32, (c, c), 1)
    rowv = lax.broadcasted_iota(jnp.int32, (c, HEAD_DIM), 0)
    row3 = lax.broadcasted_iota(jnp.int32, (nblk, DIAG, HEAD_DIM), 1)

    def level_map(rev):
        lv = jnp.zeros((c, c), jnp.int32)
        m = c
        while m > 1:
            half = m // 2
            t_right = (row % m) >= half
            s_right = (col % m) >= half
            pair = jnp.logical_and(jnp.logical_not(t_right), s_right) if rev else \
                jnp.logical_and(t_right, jnp.logical_not(s_right))
            lv = jnp.where(jnp.logical_and((row // m) == (col // m), pair), m, lv)
            m = half
        return lv

    lv_f, lv_b = level_map(False), level_map(True)
    t_rows_of = {}
    m = c
    while m > 1:
        in_right = (rowv % m) >= m // 2
        t_rows_of[(False, m)] = in_right
        t_rows_of[(True, m)] = jnp.logical_not(in_right)
        m //= 2

    far_of = {False: rowv >= hb, True: rowv < hb}
    same_half = (row // hb) == (col // hb)
    diag_of = {False: jnp.logical_and(same_half, col <= row), True: jnp.logical_and(same_half, col >= row)}
    off_of = {False: jnp.logical_and(row >= hb, col < hb), True: jnp.logical_and(row < hb, col >= hb)}

    def one_chunk(rev, cidx, st, accumulate, hh, fast):
        r0 = pl.multiple_of(cidx * c, c)
        hs = slice(hh * HEAD_DIM, (hh + 1) * HEAD_DIM)
        loc = (cum_b if rev else cum_f)[pl.ds(r0, c), hs]
        kc = (k_b if rev else k_f)[pl.ds(r0, c), hs]
        vc = vs[pl.ds(r0, c), hs]
        vcb = vc.astype(BF16)
        far = far_of[rev]
        first = loc[hb:hb + 1, :] if rev else loc[hb - 1:hb, :]
        end = loc[0:1, :] if rev else loc[c - 1:c, :]
        tot = first + end
        e_first = jnp.exp(first)

        def put(o):
            if accumulate:
                o_ref[0, pl.ds(r0, c), hs] = o_ref[0, pl.ds(r0, c), hs] + o
            else:
                o_ref[0, pl.ds(r0, c), hs] = o

        if fast:
            k_loc = kc * jnp.exp(-loc)
            if with_out:
                q_loc = qs[pl.ds(r0, c), hs] * jnp.exp(loc)
                o = _dot_nt((q_loc * jnp.where(far, e_first, 1.0)).astype(BF16), st.astype(BF16))
                q_b = q_loc.astype(BF16)
                a_diag = _dot_nt(q_b, k_loc.astype(BF16))
                a_off = _dot_nt(q_b, jnp.where(far, 0.0, k_loc * e_first).astype(BF16))
                att = jnp.where(diag_of[rev], a_diag, jnp.where(off_of[rev], a_off, 0.0))
                put(o + jnp.dot(att.astype(BF16), vcb, preferred_element_type=F32))
            e_end = jnp.exp(end)
            khat = k_loc * jnp.where(far, e_end, e_end * e_first)
            return (e_end * e_first) * st + _dot_tn(vcb, khat.astype(BF16))

        cum = loc + jnp.where(far, first, 0.0)
        if with_out:
            qc = qs[pl.ds(r0, c), hs]
            o = _dot_nt((qc * jnp.exp(cum)).astype(BF16), st.astype(BF16))
            lv = lv_b if rev else lv_f
            c3 = cum.reshape(nblk, DIAG, HEAD_DIM)
            att = jnp.zeros((c, c), F32)
            m = c
            while m > 1:
                half = m // 2
                t_rows = t_rows_of[(rev, m)]
                if m > DIAG:
                    pieces = []
                    for blk in range(c // m):
                        rr = blk * m + (half if rev else half - 1)
                        pieces.append(jnp.broadcast_to(cum[rr:rr + 1, :], (m, HEAD_DIM)))
                    ref = pieces[0] if len(pieces) == 1 else jnp.concatenate(pieces, axis=0)
                elif m > 2:
                    ref3 = None
                    for grp in reversed(range(DIAG // m)):
                        rr = grp * m + (half if rev else half - 1)
                        piece = jnp.broadcast_to(c3[:, rr:rr + 1, :], (nblk, DIAG, HEAD_DIM))
                        ref3 = piece if ref3 is None else jnp.where(row3 < (grp + 1) * m, piece, ref3)
                    ref = ref3.reshape(c, HEAD_DIM)
                else:
                    ref = jnp.where(t_rows, pltpu.roll(cum, (c - 1) if rev else 1, 0), cum)
                w = jnp.where(t_rows, qc, kc) * jnp.exp(_neg_abs(cum - ref))
                wb = w.astype(BF16)
                att = jnp.where(lv == m, _dot_nt(wb, wb), att)
                m = half
            o = o + jnp.dot(att.astype(BF16), vcb, preferred_element_type=F32)
            put(o + jnp.sum(qc * kc, axis=-1, keepdims=True) * vc)

        khat = kc * jnp.exp(tot - cum)
        return jnp.exp(tot) * st + _dot_tn(vcb, khat.astype(BF16))

    def make_body(accumulate, fast):
        def body(ci, carry):
            st_f = tuple(one_chunk(False, ci, carry[hh], accumulate, hh, fast) for hh in range(hpb))
            st_b = tuple(one_chunk(True, nchunks - 1 - ci, carry[hpb + hh], accumulate, hh, fast)
                         for hh in range(hpb))
            return st_f + st_b
        return body

    def sweep(fast):
        carry = tuple(s0f_ref[0, hh] for hh in range(hpb)) + tuple(s0b_ref[0, hh] for hh in range(hpb))
        carry = lax.fori_loop(0, nchunks // 2, make_body(False, fast), carry)
        carry = lax.fori_loop(nchunks // 2, nchunks, make_body(True, fast), carry)
        for hh in range(hpb):
            sf_ref[0, hh] = carry[hh]
            sb_ref[0, hh] = carry[hpb + hh]

    in_range = lowest >= -SAFE_DECAY

    @pl.when(in_range)
    def _():
        sweep(True)

    @pl.when(jnp.logical_not(in_range))
    def _():
        sweep(False)


def _gla(p, lb, s0f, s0b, *, with_out, col0):
    b, n, _ = p.shape
    assert n % (2 * CHUNK) == 0
    hpb = GLA_HEADS_PER_STEP
    groups = HEADS // hpb
    wblk = hpb * HEAD_DIM
    assert col0 % hpb == 0

    def col_spec(k):
        return pl.BlockSpec((1, n, wblk), lambda bi, h, k=k: (bi, 0, col0 // hpb + k * groups + h))

    st_spec = pl.BlockSpec((1, hpb, HEAD_DIM, HEAD_DIM), lambda bi, h: (bi, h, 0, 0))
    n_cols = 4 if with_out else 3
    in_specs = [col_spec(k) for k in range(n_cols)] + [
        pl.BlockSpec((2, wblk), lambda bi, h: (0, h)), st_spec, st_spec]
    st_shape = jax.ShapeDtypeStruct((b, HEADS, HEAD_DIM, HEAD_DIM), F32)
    out_specs = [st_spec, st_spec]
    out_shape = [st_shape, st_shape]
    scratch = [pltpu.VMEM((n, wblk), F32) for _ in range(5)]
    if with_out:
        out_specs = [pl.BlockSpec((1, n, wblk), lambda bi, h: (bi, 0, h))] + out_specs
        out_shape = [jax.ShapeDtypeStruct((b, n, D_HGRN), F32)] + out_shape
        scratch = [pltpu.VMEM((n, wblk), F32)] + scratch
    return pl.pallas_call(
        functools.partial(_gla_kernel, n=n, with_out=with_out, hpb=hpb),
        grid=(b, groups),
        in_specs=in_specs,
        out_specs=out_specs,
        out_shape=out_shape,
        scratch_shapes=scratch,
        compiler_params=_cparams(("parallel", "parallel")),
        name="gla" if with_out else "gla_state",
    )(*([p] * n_cols), lb, s0f, s0b)


def _pool_kernel(u_ref, pw_ref, ps_ref, o_ref, *, n, width):
    rows = n // width
    t = lax.broadcasted_iota(jnp.int32, (n, POOL_GROUP), 0)
    colp = t % width
    rowp = t // width

    def shift(a, delta, pos, length, stride):
        rolled = pltpu.roll(a, (delta * stride) % n, 0)
        ok = jnp.logical_and(pos - delta >= 0, pos - delta < length)
        return jnp.where(ok, rolled, 0.0)

    def box_sum(a, w, pos, length, stride):
        trail, lead = a, a
        h = 1
        while h < w // 2:
            trail = trail + shift(trail, h, pos, length, stride)
            lead = lead + shift(lead, -h, pos, length, stride)
            h *= 2
        return shift(trail, 1, pos, length, stride) + lead

    def count(pos, w, length):
        lo = jnp.clip(pos - w // 2, 0, length)
        hi = jnp.clip(pos - w // 2 + w, 0, length)
        return (hi - lo).astype(F32)

    for gi, w in enumerate(POOL_WINDOWS):
        ug = u_ref[0, :, gi * POOL_GROUP:(gi + 1) * POOL_GROUP].astype(F32)
        if rows > 1:
            s = box_sum(ug, w, rowp, rows, width) / count(rowp, w, rows)
            s = box_sum(s, w, colp, width, 1) / count(colp, w, width)
        else:
            s = box_sum(ug, w, colp, width, 1) / count(colp, w, width)
        y = jnp.dot((s - ug).astype(BF16), pw_ref[gi], preferred_element_type=F32)
        y = y * ps_ref[:, gi * POOL_GROUP:(gi + 1) * POOL_GROUP]
        o_ref[0, :, gi * POOL_GROUP:(gi + 1) * POOL_GROUP] = y.astype(o_ref.dtype)


def _pool(p, pool_w, pool_scale, *, on_grid, col_block):
    b, n, _ = p.shape
    width = GRID_W if on_grid else n
    assert n % width == 0 and width & (width - 1) == 0
    return pl.pallas_call(
        functools.partial(_pool_kernel, n=n, width=width),
        grid=(b,),
        in_specs=[
            pl.BlockSpec((1, n, D_POOL), lambda bi: (bi, 0, col_block)),
            pl.BlockSpec(pool_w.shape, lambda bi: (0, 0, 0)),
            pl.BlockSpec((1, D_POOL), lambda bi: (0, 0)),
        ],
        out_specs=pl.BlockSpec((1, n, D_POOL), lambda bi: (bi, 0, 0)),
        out_shape=jax.ShapeDtypeStruct((b, n, D_POOL), BF16),
        compiler_params=_cparams(("parallel",)),
        name="pool",
    )(p, pool_w, pool_scale.reshape(1, D_POOL))


def _merge_kernel(*refs, moe, n_experts):
    (o_ref, g_ref, yp_ref, ga_ref, gb_ref, x_ref, mod_ref, og_ref, post_ref, pre2_ref,
     wr_ref, wp_ref, wo_ref) = refs[:13]
    if moe:
        router_ref, xn_ref, hrow_ref, route_ref = refs[13:]
    else:
        xn_ref, h2_ref = refs[13:]
    tm = x_ref.shape[1]
    m = mod_ref[0]
    o = o_ref[0]
    parts = []
    for h in range(HEADS):
        oh = o[:, h * HEAD_DIM:(h + 1) * HEAD_DIM]
        parts.append(oh * lax.rsqrt(jnp.mean(oh * oh, axis=-1, keepdims=True) + EPS))
    on = jnp.concatenate(parts, axis=-1) * og_ref[...]
    y_rec = (on * _silu(g_ref[0].astype(F32))).astype(BF16)
    rec = jnp.dot(y_rec, wr_ref[...], preferred_element_type=F32)
    pool = jnp.dot(yp_ref[0], wp_ref[...], preferred_element_type=F32)
    merged = _sigmoid(ga_ref[0].astype(F32)) * rec + _sigmoid(gb_ref[0].astype(F32)) * pool
    y = jnp.dot(merged.astype(BF16), wo_ref[...], preferred_element_type=F32)
    xn = x_ref[0] + m[2:3] * _rms(y, post_ref[...])
    xn_ref[0] = xn
    h2 = _rms(xn, pre2_ref[...]) * (1.0 + m[4:5]) + m[3:4]
    if not moe:
        h2_ref[0] = h2.astype(h2_ref.dtype)
        return
    d = h2.shape[-1]
    for s in range(d // LANES):
        hrow_ref[pl.ds(s, tm, stride=d // LANES), :] = h2[:, s * LANES:(s + 1) * LANES]
    h_hi = h2.astype(BF16)
    h_lo = (h2 - h_hi.astype(F32)).astype(BF16)
    l2 = jnp.dot(h_hi, router_ref[...], preferred_element_type=F32)
    logits = (l2[:, :LANES] + l2[:, LANES:]
              + jnp.dot(h_lo, router_ref[:, :LANES], preferred_element_type=F32))
    lane = lax.broadcasted_iota(jnp.int32, logits.shape, 1)
    neg = jnp.float32(-jnp.inf)
    lg = jnp.where(lane < n_experts, logits, neg)
    m1 = jnp.max(lg, axis=-1, keepdims=True)
    i1 = jnp.min(jnp.where(lg == m1, lane, LANES), axis=-1, keepdims=True)
    lg2 = jnp.where(lane == i1, neg, lg)
    m2 = jnp.max(lg2, axis=-1, keepdims=True)
    i2 = jnp.min(jnp.where(lg2 == m2, lane, LANES), axis=-1, keepdims=True)
    e2 = jnp.exp(m2 - m1)
    p1 = 1.0 / (1.0 + e2)
    p2 = e2 * p1
    route = jnp.where(lane == 0, i1.astype(F32),
                      jnp.where(lane == 1, i2.astype(F32),
                                jnp.where(lane == 2, p1, jnp.where(lane == 3, p2, 0.0))))
    route_ref[...] = route


def _merge(o, p, ypool, xs, mod_l, mod_row, o_gain, post_gain, pre2_gain, w_rec, w_pool, w_out,
           router=None):
    b, n, d = xs.shape
    moe = router is not None
    tm = _pick_tile(n, 512, SUBLANES)
    nt = n // tm
    if mod_row is None:
        mod_map = lambda bi, i: (bi, 0, 0)
    else:
        mod_map = lambda bi, i: (mod_row, 0, 0)
    g_blk = 4 * D_HGRN // D_HGRN
    ga_col = 5 * D_HGRN + D_POOL
    assert ga_col % d == 0
    ga_blk = ga_col // d
    const = lambda bi, i: (0, 0)
    in_specs = [
        pl.BlockSpec((1, tm, D_HGRN), lambda bi, i: (bi, i, 0)),
        pl.BlockSpec((1, tm, D_HGRN), lambda bi, i: (bi, i, g_blk)),
        pl.BlockSpec((1, tm, D_POOL), lambda bi, i: (bi, i, 0)),
        pl.BlockSpec((1, tm, d), lambda bi, i: (bi, i, ga_blk)),
        pl.BlockSpec((1, tm, d), lambda bi, i: (bi, i, ga_blk + 1)),
        pl.BlockSpec((1, tm, d), lambda bi, i: (bi, i, 0)),
        pl.BlockSpec((1, 6, d), mod_map),
        pl.BlockSpec((1, D_HGRN), const),
        pl.BlockSpec((1, d), const),
        pl.BlockSpec((1, d), const),
        pl.BlockSpec((D_HGRN, d), const),
        pl.BlockSpec((D_POOL, d), const),
        pl.BlockSpec((d, d), const),
    ]
    args = [o, p, ypool, p, p, xs, mod_l, o_gain.reshape(1, D_HGRN), post_gain.reshape(1, d),
            pre2_gain.reshape(1, d), w_rec, w_pool, w_out]
    out_specs = [pl.BlockSpec((1, tm, d), lambda bi, i: (bi, i, 0))]
    out_shape = [jax.ShapeDtypeStruct((b, n, d), F32)]
    n_experts = 0
    if moe:
        n_experts = router.shape[1]
        router_pad = jnp.zeros((d, LANES), F32).at[:, :n_experts].set(router)
        r_hi = router_pad.astype(BF16)
        r_lo = (router_pad - r_hi.astype(F32)).astype(BF16)
        in_specs.append(pl.BlockSpec((d, 2 * LANES), const))
        args.append(jnp.concatenate([r_hi, r_lo], axis=1))
        sub = d // LANES
        out_specs += [pl.BlockSpec((tm * sub, LANES), lambda bi, i: (bi * nt + i, 0)),
                      pl.BlockSpec((tm, LANES), lambda bi, i: (bi * nt + i, 0))]
        out_shape += [jax.ShapeDtypeStruct((b * n * sub, LANES), F32),
                      jax.ShapeDtypeStruct((b * n, LANES), F32)]
    else:
        out_specs.append(pl.BlockSpec((1, tm, d), lambda bi, i: (bi, i, 0)))
        out_shape.append(jax.ShapeDtypeStruct((b, n, d), BF16))
    return pl.pallas_call(
        functools.partial(_merge_kernel, moe=moe, n_experts=n_experts),
        grid=(b, nt),
        in_specs=in_specs,
        out_specs=out_specs,
        out_shape=out_shape,
        compiler_params=_cparams(("parallel", "parallel")),
        name="merge_moe" if moe else "merge",
    )(*args)


def _ffn_kernel(h_ref, wg_ref, wu_ref, wd_ref, x_ref, mod_ref, post_ref, o_ref, acc):
    j = pl.program_id(1)

    @pl.when(j == 0)
    def _():
        acc[...] = jnp.zeros_like(acc)

    h = h_ref[...]
    g = jnp.dot(h, wg_ref[...], preferred_element_type=F32)
    u = jnp.dot(h, wu_ref[...], preferred_element_type=F32)
    acc[...] += jnp.dot((_silu(g) * u).astype(BF16), wd_ref[...], preferred_element_type=F32)

    @pl.when(j == pl.num_programs(1) - 1)
    def _():
        m = mod_ref[0]
        o_ref[...] = x_ref[...] + m[5:6] * _rms(acc[...], post_ref[...])


def _ffn_dense(h2, xs, mod_l, mod_row, post_gain, wg, wu, wd):
    b, n, d = xs.shape
    f = wg.shape[1]
    r = b * n
    tm = _pick_tile(n, 512, SUBLANES)
    tf = _pick_tile(f, 1408, LANES)
    tiles_per_batch = n // tm
    if mod_row is None:
        mod_map = lambda i, j: (i // tiles_per_batch, 0, 0)
    else:
        mod_map = lambda i, j: (mod_row, 0, 0)
    out = pl.pallas_call(
        _ffn_kernel,
        grid=(r // tm, f // tf),
        in_specs=[
            pl.BlockSpec((tm, d), lambda i, j: (i, 0)),
            pl.BlockSpec((d, tf), lambda i, j: (0, j)),
            pl.BlockSpec((d, tf), lambda i, j: (0, j)),
            pl.BlockSpec((tf, d), lambda i, j: (j, 0)),
            pl.BlockSpec((tm, d), lambda i, j: (i, 0)),
            pl.BlockSpec((1, 6, d), mod_map),
            pl.BlockSpec((1, d), lambda i, j: (0, 0)),
        ],
        out_specs=pl.BlockSpec((tm, d), lambda i, j: (i, 0)),
        out_shape=jax.ShapeDtypeStruct((r, d), F32),
        scratch_shapes=[pltpu.VMEM((tm, d), F32)],
        compiler_params=_cparams(("parallel", "arbitrary")),
        name="ffn_dense",
    )(h2.reshape(r, d), wg, wu, wd, xs.reshape(r, d), mod_l, post_gain.reshape(1, d))
    return out.reshape(b, n, d)


def _moe_kernel(ie_ref, it_ref, ilo_ref, ihi_ref, ifirst_ref, ivalid_ref, iflush_ref,
                idx0_ref, idx_next_ref, dst_prev_ref, h_hbm, wg_ref, wu_ref, wd_ref, y_hbm,
                xg, ys, xb, acc, gsem, ssem, *, tm, sub, rps, nf):
    w = pl.program_id(0)
    j = pl.program_id(1)
    tile = it_ref[w]
    slot = tile % 2
    oslot = 1 - slot

    def gather_copy(ids_ref, dslot, r):
        src = h_hbm.at[pl.ds(pl.multiple_of(ids_ref[0, 0, r] * sub, sub), sub), :]
        dst = xg.at[dslot, pl.ds(pl.multiple_of(r * sub, sub), sub), :]
        return pltpu.make_async_copy(src, dst, gsem.at[dslot])

    def scatter_copy(r):
        src = ys.at[oslot, pl.ds(pl.multiple_of(r * sub, sub), sub), :]
        dst = y_hbm.at[pl.ds(pl.multiple_of(dst_prev_ref[0, 0, r] * sub, sub), sub), :]
        return pltpu.make_async_copy(src, dst, ssem.at[oslot])

    def wait_gather(dslot):
        pltpu.make_async_copy(h_hbm.at[pl.ds(0, tm * sub), :], xg.at[dslot], gsem.at[dslot]).wait()

    def wait_scatter():
        pltpu.make_async_copy(ys.at[oslot], y_hbm.at[pl.ds(0, tm * sub), :], ssem.at[oslot]).wait()

    @pl.when(j == 0)
    def _():
        @pl.when(w == 0)
        def _():
            def issue(r, carry):
                gather_copy(idx0_ref, 0, r).start()
                return carry
            lax.fori_loop(0, tm, issue, 0)
            ys[...] = jnp.zeros_like(ys)
            wait_gather(0)

        @pl.when(jnp.logical_and(w > 0, ivalid_ref[jnp.maximum(w - 1, 0)] > 0))
        def _():
            wait_gather((it_ref[jnp.maximum(w - 1, 0)] + 1) % 2)

        @pl.when(ifirst_ref[w] > 0)
        def _():
            for s in range(sub):
                xb[:, s * LANES:(s + 1) * LANES] = xg[slot, pl.ds(s, tm, stride=sub), :].astype(BF16)

        acc[...] = jnp.zeros_like(acc)

        @pl.when(iflush_ref[w] > 0)
        def _():
            def issue(r, carry):
                scatter_copy(r).start()
                return carry
            lax.fori_loop(0, tm, issue, 0)
            wait_scatter()

    @pl.when(ivalid_ref[w] > 0)
    def _():
        base = j * rps
        for k in range(rps):
            gather_copy(idx_next_ref, oslot, base + k).start()
            scatter_copy(base + k).start()
        h = xb[...]
        g = jnp.dot(h, wg_ref[0], preferred_element_type=F32)
        u = jnp.dot(h, wu_ref[0], preferred_element_type=F32)
        acc[...] += jnp.dot((_silu(g) * u).astype(BF16), wd_ref[0], preferred_element_type=F32)

        @pl.when(j == nf - 1)
        def _():
            for r in range(rps * nf, tm):
                gather_copy(idx_next_ref, oslot, r).start()
                scatter_copy(r).start()
            wait_scatter()
            rowi = lax.broadcasted_iota(jnp.int32, (tm, LANES), 0)
            mine = jnp.logical_and(rowi >= ilo_ref[w], rowi < ihi_ref[w])
            for s in range(sub):
                old = ys[slot, pl.ds(s, tm, stride=sub), :]
                ys[slot, pl.ds(s, tm, stride=sub), :] = jnp.where(
                    mine, acc[:, s * LANES:(s + 1) * LANES], old)


def _combine_kernel(y_ref, route_ref, x_ref, mod_ref, post_ref, o_ref, *, tm, sub):
    rt = route_ref[...]
    stride = TOP_K * sub
    y1 = jnp.concatenate([y_ref[pl.ds(s, tm, stride=stride), :] for s in range(sub)], axis=-1)
    y2 = jnp.concatenate([y_ref[pl.ds(sub + s, tm, stride=stride), :] for s in range(sub)], axis=-1)
    y = rt[:, 2:3] * y1 + rt[:, 3:4] * y2
    m = mod_ref[0]
    o_ref[...] = x_ref[...] + m[5:6] * _rms(y, post_ref[...])


def _ffn_moe(hrows, route, xs, mod_l, mod_row, post_gain, wg, wu, wd):
    b, n, d = xs.shape
    r = b * n
    n_pairs = TOP_K * r
    n_exp, _, f = wg.shape
    sub = d // LANES
    tm = _pick_tile(n_pairs, 1024, SUBLANES)
    tf = _pick_tile(f, 512, LANES)
    nf = f // tf
    rps = tm // nf
    n_tiles = n_pairs // tm
    n_items = n_tiles + n_exp

    e_flat = route[:, :TOP_K].astype(jnp.int32).reshape(-1)
    pair_ids = jnp.arange(n_pairs, dtype=jnp.int32)
    order = jnp.sort(e_flat * n_pairs + pair_ids) % n_pairs
    counts = jnp.sum((e_flat[:, None] == jnp.arange(n_exp, dtype=jnp.int32)[None, :]).astype(jnp.int32),
                     axis=0)
    cend = jnp.cumsum(counts)
    cstart = cend - counts
    first_tile = cstart // tm
    last_tile = jnp.maximum(cend - 1, 0) // tm
    n_e = jnp.where(counts > 0, last_tile - first_tile + 1, 0)
    item_end = jnp.cumsum(n_e)
    item_start = item_end - n_e
    total = item_end[-1]
    wi = jnp.arange(n_items, dtype=jnp.int32)
    item_e = jnp.minimum(jnp.sum((wi[:, None] >= item_end[None, :]).astype(jnp.int32), axis=1), n_exp - 1)
    item_valid = (wi < total).astype(jnp.int32)
    item_tile = jnp.where(item_valid > 0, first_tile[item_e] + wi - item_start[item_e], n_tiles)
    item_lo = jnp.clip(cstart[item_e] - item_tile * tm, 0, tm) * item_valid
    item_hi = jnp.clip(cend[item_e] - item_tile * tm, 0, tm) * item_valid
    prev_tile = jnp.concatenate([jnp.full((1,), -1, jnp.int32), item_tile[:-1]])
    item_first = jnp.logical_and(item_valid > 0, item_tile != prev_tile).astype(jnp.int32)
    item_flush = (wi == total).astype(jnp.int32)
    last_e = item_e[jnp.maximum(total - 1, 0)]
    item_e = jnp.where(item_valid > 0, item_e, last_e)

    src3 = (order // TOP_K).reshape(n_tiles, 1, tm)
    dst3 = order.reshape(n_tiles, 1, tm)

    def w_col(w, j, ie, it, ilo, ihi, ifi, iva, ifl):
        return (ie[w], 0, jnp.where(iva[w] > 0, j, nf - 1))

    def w_row(w, j, ie, it, ilo, ihi, ifi, iva, ifl):
        return (ie[w], jnp.where(iva[w] > 0, j, nf - 1), 0)

    smem = pltpu.MemorySpace.SMEM
    y2 = pl.pallas_call(
        functools.partial(_moe_kernel, tm=tm, sub=sub, rps=rps, nf=nf),
        grid_spec=pltpu.PrefetchScalarGridSpec(
            num_scalar_prefetch=7,
            grid=(n_items, nf),
            in_specs=[
                pl.BlockSpec((1, 1, tm), lambda w, j, *_: (0, 0, 0), memory_space=smem),
                pl.BlockSpec((1, 1, tm), lambda w, j, ie, it, *_: (jnp.minimum(it[w] + 1, n_tiles - 1), 0, 0),
                             memory_space=smem),
                pl.BlockSpec((1, 1, tm), lambda w, j, ie, it, *_: (jnp.clip(it[w] - 1, 0, n_tiles - 1), 0, 0),
                             memory_space=smem),
                pl.BlockSpec(memory_space=pl.ANY),
                pl.BlockSpec((1, d, tf), w_col),
                pl.BlockSpec((1, d, tf), w_col),
                pl.BlockSpec((1, tf, d), w_row),
            ],
            out_specs=pl.BlockSpec(memory_space=pl.ANY),
            scratch_shapes=[
                pltpu.VMEM((2, tm * sub, LANES), F32),
                pltpu.VMEM((2, tm * sub, LANES), F32),
                pltpu.VMEM((tm, d), BF16),
                pltpu.VMEM((tm, d), F32),
                pltpu.SemaphoreType.DMA((2,)),
                pltpu.SemaphoreType.DMA((2,)),
            ],
        ),
        out_shape=jax.ShapeDtypeStruct((n_pairs * sub, LANES), F32),
        compiler_params=_cparams(("arbitrary", "arbitrary")),
        name="moe_experts",
    )(item_e, item_tile, item_lo, item_hi, item_first, item_valid, item_flush,
      src3, src3, dst3, hrows, wg, wu, wd)

    tmc = _pick_tile(n, 512, SUBLANES)
    tiles_per_batch = n // tmc
    if mod_row is None:
        mod_map = lambda i: (i // tiles_per_batch, 0, 0)
    else:
        mod_map = lambda i: (mod_row, 0, 0)
    out = pl.pallas_call(
        functools.partial(_combine_kernel, tm=tmc, sub=sub),
        grid=(r // tmc,),
        in_specs=[
            pl.BlockSpec((tmc * TOP_K * sub, LANES), lambda i: (i, 0)),
            pl.BlockSpec((tmc, LANES), lambda i: (i, 0)),
            pl.BlockSpec((tmc, d), lambda i: (i, 0)),
            pl.BlockSpec((1, 6, d), mod_map),
            pl.BlockSpec((1, d), lambda i: (0, 0)),
        ],
        out_specs=pl.BlockSpec((tmc, d), lambda i: (i, 0)),
        out_shape=jax.ShapeDtypeStruct((r, d), F32),
        compiler_params=_cparams(("parallel",)),
        name="moe_combine",
    )(y2, route, xs.reshape(r, d), mod_l, post_gain.reshape(1, d))
    return out.reshape(b, n, d)


def kernel(x, c, ctx, c_ctx, ada_w, ada_b, mix_pre, mix_post, ffn_pre, ffn_post, w_in, hgrn_lb, hgrn_gain,
           pool_w, pool_scale, w_branch_rec, w_branch_pool, w_out, ffn_w_gate, ffn_w_up, ffn_w_down,
           moe_router, moe_w_gate, moe_w_up, moe_w_down):
    b, n, d = x.shape
    depth = ada_w.shape[0]
    assert b + 1 <= MOD_ROWS and d % LANES == 0
    ctx_row = b

    cond = jnp.zeros((MOD_ROWS, d), F32).at[:b].set(c).at[b].set(c_ctx)
    mod = _modulation(cond, ada_w, ada_b).reshape(depth, MOD_ROWS, 6, d)

    lb_all = jnp.cumsum(jax.nn.softmax(hgrn_lb.astype(F32), axis=0), axis=0)
    lb_all = lb_all - lb_all[0:1]

    bf = lambda a: a.astype(BF16)
    w_in_b, pool_w_b = bf(w_in), bf(pool_w)
    w_rec_b, w_pool_b, w_out_b = bf(w_branch_rec), bf(w_branch_pool), bf(w_out)
    ffn_g, ffn_u, ffn_d = bf(ffn_w_gate), bf(ffn_w_up), bf(ffn_w_down)
    moe_g, moe_u, moe_d = bf(moe_w_gate), bf(moe_w_up), bf(moe_w_down)

    u_blk = (5 * D_HGRN) // D_POOL
    s_zero = jnp.zeros((b, HEADS, HEAD_DIM, HEAD_DIM), F32)

    def mixer(xs, l, mod_row, s0f, s0b, on_grid, router):
        p = _in_proj(xs, mod[l], mod_row, mix_pre[l], w_in_b[l])
        o, s_f, s_b = _gla(p, lb_all[l], s0f, s0b, with_out=True, col0=0)
        yp = _pool(p, pool_w_b[l], pool_scale[l], on_grid=on_grid, col_block=u_blk)
        outs = _merge(o, p, yp, xs, mod[l], mod_row, hgrn_gain[l], mix_post[l], ffn_pre[l],
                      w_rec_b[l], w_pool_b[l], w_out_b[l], router=router)
        return outs, s_f, s_b

    def channel(outs, l, mod_row):
        j = l // 2
        if l % 2 == 0:
            xn, h2 = outs
            return _ffn_dense(h2, xn, mod[l], mod_row, ffn_post[l], ffn_g[j], ffn_u[j], ffn_d[j])
        xn, hrows, route = outs
        return _ffn_moe(hrows, route, xn, mod[l], mod_row, ffn_post[l], moe_g[j], moe_u[j], moe_d[j])

    cx = ctx
    for l in range(depth):
        last = l == depth - 1
        router = moe_router[l // 2] if l % 2 == 1 else None
        if last:
            pc = _in_proj(cx, mod[l], ctx_row, mix_pre[l], w_in_b[l][:, D_HGRN:4 * D_HGRN])
            s_f, s_b = _gla(pc, lb_all[l], s_zero, s_zero, with_out=False, col0=0)
            c_outs = None
        else:
            c_outs, s_f, s_b = mixer(cx, l, ctx_row, s_zero, s_zero, False, router)
        x_outs, _, _ = mixer(x, l, None, s_f, s_b, True, router)
        x = channel(x_outs, l, None)
        if not last:
            cx = channel(c_outs, l, ctx_row)
    return x
```

```python
import functools

import jax
import jax.numpy as jnp
from jax import lax
from jax.experimental import pallas as pl
from jax.experimental.pallas import tpu as pltpu

F32 = jnp.float32
BF16 = jnp.bfloat16

EPS = 1e-6
LANES = 128
SUBLANES = 8
HEADS = 4
HEAD_DIM = 128
D_HGRN = HEADS * HEAD_DIM
D_POOL = 512
POOL_WINDOWS = (2, 4, 8, 16)
POOL_GROUP = D_POOL // len(POOL_WINDOWS)
GRID_W = 64
CHUNK = 64
SUBCHUNK = CHUNK // 2
SAFE_DECAY = 85.0
DIAG = SUBLANES
GLA_HEADS_PER_STEP = 2
TOP_K = 2
MOD_ROWS = 40
VMEM_LIMIT = 56 * 1024 * 1024

def _cparams(sem):
    return pltpu.CompilerParams(dimension_semantics=sem, vmem_limit_bytes=VMEM_LIMIT)


def _pick_tile(n, cap, mult):
    if n <= cap:
        return n
    best = None
    for t in range(mult, cap + 1, mult):
        if n % t == 0:
            best = t
    assert best is not None, (n, cap, mult)
    return best


def _sigmoid(z):
    return 1.0 / (1.0 + jnp.exp(-z))


def _silu(z):
    return z * _sigmoid(z)


def _mod_kernel(c_ref, w_ref, b_ref, o_ref):
    a = _silu(c_ref[...])
    o_ref[0] = jnp.dot(a, w_ref[0], preferred_element_type=F32,
                       precision=lax.Precision.HIGHEST) + b_ref[0]


def _modulation(cond, ada_w, ada_b):
    depth, d, six_d = ada_w.shape
    tn = _pick_tile(six_d, 1024, LANES)
    return pl.pallas_call(
        _mod_kernel,
        grid=(depth, six_d // tn),
        in_specs=[
            pl.BlockSpec((MOD_ROWS, d), lambda l, j: (0, 0)),
            pl.BlockSpec((1, d, tn), lambda l, j: (l, 0, j)),
            pl.BlockSpec((1, 1, tn), lambda l, j: (l, 0, j)),
        ],
        out_specs=pl.BlockSpec((1, MOD_ROWS, tn), lambda l, j: (l, 0, j)),
        out_shape=jax.ShapeDtypeStruct((depth, MOD_ROWS, six_d), F32),
        compiler_params=_cparams(("parallel", "parallel")),
        name="modulation",
    )(cond, ada_w, ada_b.reshape(depth, 1, six_d))


def _rms(x, gain):
    ms = jnp.mean(x * x, axis=-1, keepdims=True)
    return x * lax.rsqrt(ms + EPS) * gain


def _inproj_kernel(x_ref, mod_ref, gain_ref, w_ref, o_ref, *, tn):
    m = mod_ref[0]
    h = _rms(x_ref[0], gain_ref[...]) * (1.0 + m[1:2]) + m[0:1]
    hb = h.astype(BF16)
    for j in range(w_ref.shape[1] // tn):
        o_ref[0, :, j * tn:(j + 1) * tn] = jnp.dot(
            hb, w_ref[:, j * tn:(j + 1) * tn], preferred_element_type=F32).astype(o_ref.dtype)


def _in_proj(xs, mod_l, mod_row, gain, w):
    b, n, d = xs.shape
    nc = w.shape[1]
    tm = _pick_tile(n, 512, SUBLANES)
    tn = _pick_tile(nc, 512, LANES)
    if mod_row is None:
        mod_map = lambda bi, i: (bi, 0, 0)
    else:
        mod_map = lambda bi, i: (mod_row, 0, 0)
    return pl.pallas_call(
        functools.partial(_inproj_kernel, tn=tn),
        grid=(b, n // tm),
        in_specs=[
            pl.BlockSpec((1, tm, d), lambda bi, i: (bi, i, 0)),
            pl.BlockSpec((1, 6, d), mod_map),
            pl.BlockSpec((1, d), lambda bi, i: (0, 0)),
            pl.BlockSpec((d, nc), lambda bi, i: (0, 0)),
        ],
        out_specs=pl.BlockSpec((1, tm, nc), lambda bi, i: (bi, i, 0)),
        out_shape=jax.ShapeDtypeStruct((b, n, nc), BF16),
        compiler_params=_cparams(("parallel", "parallel")),
        name="in_proj",
    )(xs, mod_l, gain.reshape(1, d), w)


def _neg_abs(a):
    bits = lax.bitcast_convert_type(a, jnp.uint32) | jnp.uint32(0x80000000)
    return lax.bitcast_convert_type(bits, F32)


def _dot_nt(a, b):
    return lax.dot_general(a, b, (((1,), (1,)), ((), ())), preferred_element_type=F32)


def _dot_tn(a, b):
    return lax.dot_general(a, b, (((0,), (0,)), ((), ())), preferred_element_type=F32)


def _gla_kernel(*refs, n, with_out, hpb):
    if with_out:
        (q_ref, ff_ref, fb_ref, i_ref, lb_ref, s0f_ref, s0b_ref,
         o_ref, sf_ref, sb_ref, qs, qf_f, qf_b, vs, cum_f, cum_b, k_f, k_b, kh_f, kh_b) = refs
    else:
        (ff_ref, fb_ref, i_ref, lb_ref, s0f_ref, s0b_ref,
         sf_ref, sb_ref, vs, cum_f, cum_b, k_f, k_b, kh_f, kh_b) = refs
    c = CHUNK
    nchunks = n // c
    nblk = c // DIAG

    hb = SUBCHUNK
    pos_c = lax.broadcasted_iota(jnp.int32, (c, hpb * HEAD_DIM), 0) % hb

    def prepare(ci, low):
        rows = pl.ds(pl.multiple_of(ci * c, c), c)
        if with_out:
            qs[rows, :] = _silu(q_ref[0, rows, :].astype(F32)) * (HEAD_DIM ** -0.5)
        vs[rows, :] = i_ref[0, rows, :].astype(F32)
        for rev in (False, True):
            z = (fb_ref if rev else ff_ref)[0, rows, :].astype(F32)
            lb = lb_ref[1:2, :] if rev else lb_ref[0:1, :]
            e = jnp.exp(-jnp.abs(z))
            log_sig = jnp.minimum(z, 0.0) - jnp.log(1.0 + e)
            a = jnp.log(lb)
            bt = jnp.log(1.0 - lb) + log_sig
            mx = jnp.maximum(a, bt)
            mn = jnp.minimum(a, bt)
            s = mx + jnp.log(1.0 + jnp.exp(mn - mx))
            (k_b if rev else k_f)[rows, :] = (1.0 - lb) * (jnp.where(z >= 0.0, e, 1.0) / (1.0 + e))
            sh = 1
            while sh < hb:
                if rev:
                    s = s + jnp.where(pos_c < hb - sh, pltpu.roll(s, c - sh, 0), 0.0)
                else:
                    s = s + jnp.where(pos_c >= sh, pltpu.roll(s, sh, 0), 0.0)
                sh *= 2
            (cum_b if rev else cum_f)[rows, :] = s
            low = jnp.minimum(low, s)
        return low

    lowest = jnp.min(lax.fori_loop(0, nchunks, prepare, jnp.zeros((c, hpb * HEAD_DIM), F32)))

    row = lax.broadcasted_iota(jnp.int32, (c, c), 0)
    col = lax.broadcasted_iota(jnp.int32, (c, c), 1)
    rowv = lax.broadcasted_iota(jnp.int32, (c, HEAD_DIM), 0)
    row3 = lax.broadcasted_iota(jnp.int32, (nblk, DIAG, HEAD_DIM), 1)

    def level_map(rev):
        lv = jnp.zeros((c, c), jnp.int32)
        m = c
        while m > 1:
            half = m // 2
            t_right = (row % m) >= half
            s_right = (col % m) >= half
            pair = jnp.logical_and(jnp.logical_not(t_right), s_right) if rev else \
                jnp.logical_and(t_right, jnp.logical_not(s_right))
            lv = jnp.where(jnp.logical_and((row // m) == (col // m), pair), m, lv)
            m = half
        return lv

    lv_f, lv_b = level_map(False), level_map(True)
    t_rows_of = {}
    m = c
    while m > 1:
        in_right = (rowv % m) >= m // 2
        t_rows_of[(False, m)] = in_right
        t_rows_of[(True, m)] = jnp.logical_not(in_right)
        m //= 2

    far_of = {False: rowv >= hb, True: rowv < hb}

    def one_chunk(rev, cidx, st, accumulate, hh):
        r0 = pl.multiple_of(cidx * c, c)
        hs = slice(hh * HEAD_DIM, (hh + 1) * HEAD_DIM)
        loc = (cum_b if rev else cum_f)[pl.ds(r0, c), hs]
        kc = (k_b if rev else k_f)[pl.ds(r0, c), hs]
        vc = vs[pl.ds(r0, c), hs]
        vcb = vc.astype(BF16)
        far = far_of[rev]
        first = loc[hb:hb + 1, :] if rev else loc[hb - 1:hb, :]
        end = loc[0:1, :] if rev else loc[c - 1:c, :]
        tot = first + end

        def put(o):
            if accumulate:
                o_ref[0, pl.ds(r0, c), hs] = o_ref[0, pl.ds(r0, c), hs] + o
            else:
                o_ref[0, pl.ds(r0, c), hs] = o

        cum = loc + jnp.where(far, first, 0.0)
        if with_out:
            qc = qs[pl.ds(r0, c), hs]
            o = _dot_nt((qc * jnp.exp(cum)).astype(BF16), st.astype(BF16))
            lv = lv_b if rev else lv_f
            c3 = cum.reshape(nblk, DIAG, HEAD_DIM)
            att = jnp.zeros((c, c), F32)
            m = c
            while m > 1:
                half = m // 2
                t_rows = t_rows_of[(rev, m)]
                if m > DIAG:
                    pieces = []
                    for blk in range(c // m):
                        rr = blk * m + (half if rev else half - 1)
                        pieces.append(jnp.broadcast_to(cum[rr:rr + 1, :], (m, HEAD_DIM)))
                    ref = pieces[0] if len(pieces) == 1 else jnp.concatenate(pieces, axis=0)
                elif m > 2:
                    ref3 = None
                    for grp in reversed(range(DIAG // m)):
                        rr = grp * m + (half if rev else half - 1)
                        piece = jnp.broadcast_to(c3[:, rr:rr + 1, :], (nblk, DIAG, HEAD_DIM))
                        ref3 = piece if ref3 is None else jnp.where(row3 < (grp + 1) * m, piece, ref3)
                    ref = ref3.reshape(c, HEAD_DIM)
                else:
                    ref = jnp.where(t_rows, pltpu.roll(cum, (c - 1) if rev else 1, 0), cum)
                w = jnp.where(t_rows, qc, kc) * jnp.exp(_neg_abs(cum - ref))
                wb = w.astype(BF16)
                att = jnp.where(lv == m, _dot_nt(wb, wb), att)
                m = half
            o = o + jnp.dot(att.astype(BF16), vcb, preferred_element_type=F32)
            put(o + jnp.sum(qc * kc, axis=-1, keepdims=True) * vc)

        khat = kc * jnp.exp(tot - cum)
        return jnp.exp(tot) * st + _dot_tn(vcb, khat.astype(BF16))

    def make_body(accumulate):
        def body(ci, carry):
            st_f = tuple(one_chunk(False, ci, carry[hh], accumulate, hh) for hh in range(hpb))
            st_b = tuple(one_chunk(True, nchunks - 1 - ci, carry[hpb + hh], accumulate, hh)
                         for hh in range(hpb))
            return st_f + st_b
        return body

    def initial_states():
        return tuple(s0f_ref[0, hh] for hh in range(hpb)) + tuple(s0b_ref[0, hh] for hh in range(hpb))

    def put_states(carry):
        for hh in range(hpb):
            sf_ref[0, hh] = carry[hh]
            sb_ref[0, hh] = carry[hpb + hh]

    def sweep_midpoint():
        carry = lax.fori_loop(0, nchunks // 2, make_body(False), initial_states())
        put_states(lax.fori_loop(nchunks // 2, nchunks, make_body(True), carry))

    ab = 2 * c
    rowa = lax.broadcasted_iota(jnp.int32, (ab, ab), 0)
    cola = lax.broadcasted_iota(jnp.int32, (ab, ab), 1)
    rowb = lax.broadcasted_iota(jnp.int32, (ab, HEAD_DIM), 0)

    same_chunk = (rowa // c) == (cola // c)
    keep_of = {False: jnp.logical_and(same_chunk, cola <= rowa),
               True: jnp.logical_and(same_chunk, cola >= rowa)}
    far_b = {False: (rowb % c) >= hb, True: (rowb % c) < hb}

    def chunk_rows(x, r):
        return jnp.concatenate(
            [jnp.broadcast_to(x[ch * c + r:ch * c + r + 1, :], (c, HEAD_DIM)) for ch in range(ab // c)],
            axis=0)

    def att_block(rev, r0, hh):
        hs = slice(hh * HEAD_DIM, (hh + 1) * HEAD_DIM)
        loc = (cum_b if rev else cum_f)[pl.ds(r0, ab), hs]
        far = far_b[rev]
        e_first = jnp.exp(chunk_rows(loc, hb if rev else hb - 1))
        e_end = jnp.exp(chunk_rows(loc, 0 if rev else c - 1))
        k_loc = (k_b if rev else k_f)[pl.ds(r0, ab), hs] * jnp.exp(-loc)
        (kh_b if rev else kh_f)[pl.ds(r0, ab), hs] = (
            k_loc * jnp.where(far, e_end, e_end * e_first)).astype(BF16)
        if not with_out:
            return None
        q_loc = qs[pl.ds(r0, ab), hs] * jnp.exp(loc)
        (qf_b if rev else qf_f)[pl.ds(r0, ab), hs] = (q_loc * jnp.where(far, e_first, 1.0)).astype(BF16)
        q2 = jnp.concatenate([jnp.where(far, 0.0, q_loc), jnp.where(far, q_loc, 0.0)], axis=1)
        k2 = jnp.concatenate([k_loc, jnp.where(far, k_loc, k_loc * e_first)], axis=1)
        att = jnp.where(keep_of[rev], _dot_nt(q2.astype(BF16), k2.astype(BF16)), 0.0)
        return jnp.dot(att.astype(BF16), vs[pl.ds(r0, ab), hs].astype(BF16), preferred_element_type=F32)

    def att_body(bi, carry):
        r0 = pl.multiple_of(bi * ab, ab)
        for hh in range(hpb):
            o_f = att_block(False, r0, hh)
            o_b = att_block(True, r0, hh)
            if with_out:
                o_ref[0, pl.ds(r0, ab), hh * HEAD_DIM:(hh + 1) * HEAD_DIM] = o_f + o_b
        return carry

    def state_chunk(rev, cidx, st, hh):
        r0 = pl.multiple_of(cidx * c, c)
        hs = slice(hh * HEAD_DIM, (hh + 1) * HEAD_DIM)
        sums_ref = cum_b if rev else cum_f

        def row_at(off):
            grp = off // SUBLANES * SUBLANES
            blk = sums_ref[pl.ds(pl.multiple_of(r0 + grp, SUBLANES), SUBLANES), hs]
            return blk[off - grp:off - grp + 1, :]

        first = row_at(hb if rev else hb - 1)
        end = row_at(0 if rev else c - 1)
        if with_out:
            qf = (qf_b if rev else qf_f)[pl.ds(r0, c), hs]
            (k_b if rev else k_f)[pl.ds(r0, c), hs] = _dot_nt(qf, st.astype(BF16))
        kh = (kh_b if rev else kh_f)[pl.ds(r0, c), hs]
        return jnp.exp(first + end) * st + _dot_tn(vs[pl.ds(r0, c), hs].astype(BF16), kh)

    def state_body(ci, carry):
        st_f = tuple(state_chunk(False, ci, carry[hh], hh) for hh in range(hpb))
        st_b = tuple(state_chunk(True, nchunks - 1 - ci, carry[hpb + hh], hh) for hh in range(hpb))
        return st_f + st_b

    def add_state_part(ci, carry):
        rows = pl.ds(pl.multiple_of(ci * c, c), c)
        o_ref[0, rows, :] = o_ref[0, rows, :] + k_f[rows, :] + k_b[rows, :]
        return carry

    def sweep_one_sided():
        lax.fori_loop(0, n // ab, att_body, 0)
        put_states(lax.fori_loop(0, nchunks, state_body, initial_states()))
        if with_out:
            lax.fori_loop(0, nchunks, add_state_part, 0)

    in_range = lowest >= -SAFE_DECAY

    @pl.when(in_range)
    def _():
        sweep_one_sided()

    @pl.when(jnp.logical_not(in_range))
    def _():
        sweep_midpoint()


def _gla(p, lb, s0f, s0b, *, with_out, col0):
    b, n, _ = p.shape
    assert n % (2 * CHUNK) == 0
    hpb = GLA_HEADS_PER_STEP
    groups = HEADS // hpb
    wblk = hpb * HEAD_DIM
    assert col0 % hpb == 0

    def col_spec(k):
        return pl.BlockSpec((1, n, wblk), lambda bi, h, k=k: (bi, 0, col0 // hpb + k * groups + h))

    st_spec = pl.BlockSpec((1, hpb, HEAD_DIM, HEAD_DIM), lambda bi, h: (bi, h, 0, 0))
    n_cols = 4 if with_out else 3
    in_specs = [col_spec(k) for k in range(n_cols)] + [
        pl.BlockSpec((2, wblk), lambda bi, h: (0, h)), st_spec, st_spec]
    st_shape = jax.ShapeDtypeStruct((b, HEADS, HEAD_DIM, HEAD_DIM), F32)
    out_specs = [st_spec, st_spec]
    out_shape = [st_shape, st_shape]
    scratch = [pltpu.VMEM((n, wblk), F32) for _ in range(5)] + [pltpu.VMEM((n, wblk), BF16)] * 2
    if with_out:
        out_specs = [pl.BlockSpec((1, n, wblk), lambda bi, h: (bi, 0, h))] + out_specs
        out_shape = [jax.ShapeDtypeStruct((b, n, D_HGRN), F32)] + out_shape
        scratch = [pltpu.VMEM((n, wblk), F32)] + [pltpu.VMEM((n, wblk), BF16)] * 2 + scratch
    return pl.pallas_call(
        functools.partial(_gla_kernel, n=n, with_out=with_out, hpb=hpb),
        grid=(b, groups),
        in_specs=in_specs,
        out_specs=out_specs,
        out_shape=out_shape,
        scratch_shapes=scratch,
        compiler_params=_cparams(("parallel", "parallel")),
        name="gla" if with_out else "gla_state",
    )(*([p] * n_cols), lb, s0f, s0b)


def _pool_kernel(u_ref, pw_ref, ps_ref, o_ref, *, n, width):
    rows = n // width
    t = lax.broadcasted_iota(jnp.int32, (n, POOL_GROUP), 0)
    colp = t % width
    rowp = t // width

    def shift(a, delta, pos, length, stride):
        rolled = pltpu.roll(a, (delta * stride) % n, 0)
        ok = jnp.logical_and(pos - delta >= 0, pos - delta < length)
        return jnp.where(ok, rolled, 0.0)

    def box_sum(a, w, pos, length, stride):
        trail, lead = a, a
        h = 1
        while h < w // 2:
            trail = trail + shift(trail, h, pos, length, stride)
            lead = lead + shift(lead, -h, pos, length, stride)
            h *= 2
        return shift(trail, 1, pos, length, stride) + lead

    def count(pos, w, length):
        lo = jnp.clip(pos - w // 2, 0, length)
        hi = jnp.clip(pos - w // 2 + w, 0, length)
        return (hi - lo).astype(F32)

    for gi, w in enumerate(POOL_WINDOWS):
        ug = u_ref[0, :, gi * POOL_GROUP:(gi + 1) * POOL_GROUP].astype(F32)
        if rows > 1:
            s = box_sum(ug, w, rowp, rows, width) / count(rowp, w, rows)
            s = box_sum(s, w, colp, width, 1) / count(colp, w, width)
        else:
            s = box_sum(ug, w, colp, width, 1) / count(colp, w, width)
        y = jnp.dot((s - ug).astype(BF16), pw_ref[gi], preferred_element_type=F32)
        y = y * ps_ref[:, gi * POOL_GROUP:(gi + 1) * POOL_GROUP]
        o_ref[0, :, gi * POOL_GROUP:(gi + 1) * POOL_GROUP] = y.astype(o_ref.dtype)


def _pool(p, pool_w, pool_scale, *, on_grid, col_block):
    b, n, _ = p.shape
    width = GRID_W if on_grid else n
    assert n % width == 0 and width & (width - 1) == 0
    return pl.pallas_call(
        functools.partial(_pool_kernel, n=n, width=width),
        grid=(b,),
        in_specs=[
            pl.BlockSpec((1, n, D_POOL), lambda bi: (bi, 0, col_block)),
            pl.BlockSpec(pool_w.shape, lambda bi: (0, 0, 0)),
            pl.BlockSpec((1, D_POOL), lambda bi: (0, 0)),
        ],
        out_specs=pl.BlockSpec((1, n, D_POOL), lambda bi: (bi, 0, 0)),
        out_shape=jax.ShapeDtypeStruct((b, n, D_POOL), BF16),
        compiler_params=_cparams(("parallel",)),
        name="pool",
    )(p, pool_w, pool_scale.reshape(1, D_POOL))


def _merge_kernel(*refs, moe, n_experts):
    (o_ref, g_ref, yp_ref, ga_ref, gb_ref, x_ref, mod_ref, og_ref, post_ref, pre2_ref,
     wr_ref, wp_ref, wo_ref) = refs[:13]
    if moe:
        router_ref, xn_ref, hrow_ref, route_ref = refs[13:]
    else:
        xn_ref, h2_ref = refs[13:]
    tm = x_ref.shape[1]
    m = mod_ref[0]
    o = o_ref[0]
    parts = []
    for h in range(HEADS):
        oh = o[:, h * HEAD_DIM:(h + 1) * HEAD_DIM]
        parts.append(oh * lax.rsqrt(jnp.mean(oh * oh, axis=-1, keepdims=True) + EPS))
    on = jnp.concatenate(parts, axis=-1) * og_ref[...]
    y_rec = (on * _silu(g_ref[0].astype(F32))).astype(BF16)
    rec = jnp.dot(y_rec, wr_ref[...], preferred_element_type=F32)
    pool = jnp.dot(yp_ref[0], wp_ref[...], preferred_element_type=F32)
    merged = _sigmoid(ga_ref[0].astype(F32)) * rec + _sigmoid(gb_ref[0].astype(F32)) * pool
    y = jnp.dot(merged.astype(BF16), wo_ref[...], preferred_element_type=F32)
    xn = x_ref[0] + m[2:3] * _rms(y, post_ref[...])
    xn_ref[0] = xn
    h2 = _rms(xn, pre2_ref[...]) * (1.0 + m[4:5]) + m[3:4]
    if not moe:
        h2_ref[0] = h2.astype(h2_ref.dtype)
        return
    d = h2.shape[-1]
    for s in range(d // LANES):
        hrow_ref[pl.ds(s, tm, stride=d // LANES), :] = h2[:, s * LANES:(s + 1) * LANES]
    h_hi = h2.astype(BF16)
    h_lo = (h2 - h_hi.astype(F32)).astype(BF16)
    l2 = jnp.dot(h_hi, router_ref[...], preferred_element_type=F32)
    logits = (l2[:, :LANES] + l2[:, LANES:]
              + jnp.dot(h_lo, router_ref[:, :LANES], preferred_element_type=F32))
    lane = lax.broadcasted_iota(jnp.int32, logits.shape, 1)
    neg = jnp.float32(-jnp.inf)
    lg = jnp.where(lane < n_experts, logits, neg)
    m1 = jnp.max(lg, axis=-1, keepdims=True)
    i1 = jnp.min(jnp.where(lg == m1, lane, LANES), axis=-1, keepdims=True)
    lg2 = jnp.where(lane == i1, neg, lg)
    m2 = jnp.max(lg2, axis=-1, keepdims=True)
    i2 = jnp.min(jnp.where(lg2 == m2, lane, LANES), axis=-1, keepdims=True)
    e2 = jnp.exp(m2 - m1)
    p1 = 1.0 / (1.0 + e2)
    p2 = e2 * p1
    route = jnp.where(lane == 0, i1.astype(F32),
                      jnp.where(lane == 1, i2.astype(F32),
                                jnp.where(lane == 2, p1, jnp.where(lane == 3, p2, 0.0))))
    route_ref[...] = route


def _merge(o, p, ypool, xs, mod_l, mod_row, o_gain, post_gain, pre2_gain, w_rec, w_pool, w_out,
           router=None):
    b, n, d = xs.shape
    moe = router is not None
    tm = _pick_tile(n, 512, SUBLANES)
    nt = n // tm
    if mod_row is None:
        mod_map = lambda bi, i: (bi, 0, 0)
    else:
        mod_map = lambda bi, i: (mod_row, 0, 0)
    g_blk = 4 * D_HGRN // D_HGRN
    ga_col = 5 * D_HGRN + D_POOL
    assert ga_col % d == 0
    ga_blk = ga_col // d
    const = lambda bi, i: (0, 0)
    in_specs = [
        pl.BlockSpec((1, tm, D_HGRN), lambda bi, i: (bi, i, 0)),
        pl.BlockSpec((1, tm, D_HGRN), lambda bi, i: (bi, i, g_blk)),
        pl.BlockSpec((1, tm, D_POOL), lambda bi, i: (bi, i, 0)),
        pl.BlockSpec((1, tm, d), lambda bi, i: (bi, i, ga_blk)),
        pl.BlockSpec((1, tm, d), lambda bi, i: (bi, i, ga_blk + 1)),
        pl.BlockSpec((1, tm, d), lambda bi, i: (bi, i, 0)),
        pl.BlockSpec((1, 6, d), mod_map),
        pl.BlockSpec((1, D_HGRN), const),
        pl.BlockSpec((1, d), const),
        pl.BlockSpec((1, d), const),
        pl.BlockSpec((D_HGRN, d), const),
        pl.BlockSpec((D_POOL, d), const),
        pl.BlockSpec((d, d), const),
    ]
    args = [o, p, ypool, p, p, xs, mod_l, o_gain.reshape(1, D_HGRN), post_gain.reshape(1, d),
            pre2_gain.reshape(1, d), w_rec, w_pool, w_out]
    out_specs = [pl.BlockSpec((1, tm, d), lambda bi, i: (bi, i, 0))]
    out_shape = [jax.ShapeDtypeStruct((b, n, d), F32)]
    n_experts = 0
    if moe:
        n_experts = router.shape[1]
        router_pad = jnp.zeros((d, LANES), F32).at[:, :n_experts].set(router)
        r_hi = router_pad.astype(BF16)
        r_lo = (router_pad - r_hi.astype(F32)).astype(BF16)
        in_specs.append(pl.BlockSpec((d, 2 * LANES), const))
        args.append(jnp.concatenate([r_hi, r_lo], axis=1))
        sub = d // LANES
        out_specs += [pl.BlockSpec((tm * sub, LANES), lambda bi, i: (bi * nt + i, 0)),
                      pl.BlockSpec((tm, LANES), lambda bi, i: (bi * nt + i, 0))]
        out_shape += [jax.ShapeDtypeStruct((b * n * sub, LANES), F32),
                      jax.ShapeDtypeStruct((b * n, LANES), F32)]
    else:
        out_specs.append(pl.BlockSpec((1, tm, d), lambda bi, i: (bi, i, 0)))
        out_shape.append(jax.ShapeDtypeStruct((b, n, d), BF16))
    return pl.pallas_call(
        functools.partial(_merge_kernel, moe=moe, n_experts=n_experts),
        grid=(b, nt),
        in_specs=in_specs,
        out_specs=out_specs,
        out_shape=out_shape,
        compiler_params=_cparams(("parallel", "parallel")),
        name="merge_moe" if moe else "merge",
    )(*args)


def _ffn_kernel(h_ref, wg_ref, wu_ref, wd_ref, x_ref, mod_ref, post_ref, o_ref, acc):
    j = pl.program_id(1)

    @pl.when(j == 0)
    def _():
        acc[...] = jnp.zeros_like(acc)

    h = h_ref[...]
    g = jnp.dot(h, wg_ref[...], preferred_element_type=F32)
    u = jnp.dot(h, wu_ref[...], preferred_element_type=F32)
    acc[...] += jnp.dot((_silu(g) * u).astype(BF16), wd_ref[...], preferred_element_type=F32)

    @pl.when(j == pl.num_programs(1) - 1)
    def _():
        m = mod_ref[0]
        o_ref[...] = x_ref[...] + m[5:6] * _rms(acc[...], post_ref[...])


def _ffn_dense(h2, xs, mod_l, mod_row, post_gain, wg, wu, wd):
    b, n, d = xs.shape
    f = wg.shape[1]
    r = b * n
    tm = _pick_tile(n, 512, SUBLANES)
    tf = _pick_tile(f, 1408, LANES)
    tiles_per_batch = n // tm
    if mod_row is None:
        mod_map = lambda i, j: (i // tiles_per_batch, 0, 0)
    else:
        mod_map = lambda i, j: (mod_row, 0, 0)
    out = pl.pallas_call(
        _ffn_kernel,
        grid=(r // tm, f // tf),
        in_specs=[
            pl.BlockSpec((tm, d), lambda i, j: (i, 0)),
            pl.BlockSpec((d, tf), lambda i, j: (0, j)),
            pl.BlockSpec((d, tf), lambda i, j: (0, j)),
            pl.BlockSpec((tf, d), lambda i, j: (j, 0)),
            pl.BlockSpec((tm, d), lambda i, j: (i, 0)),
            pl.BlockSpec((1, 6, d), mod_map),
            pl.BlockSpec((1, d), lambda i, j: (0, 0)),
        ],
        out_specs=pl.BlockSpec((tm, d), lambda i, j: (i, 0)),
        out_shape=jax.ShapeDtypeStruct((r, d), F32),
        scratch_shapes=[pltpu.VMEM((tm, d), F32)],
        compiler_params=_cparams(("parallel", "arbitrary")),
        name="ffn_dense",
    )(h2.reshape(r, d), wg, wu, wd, xs.reshape(r, d), mod_l, post_gain.reshape(1, d))
    return out.reshape(b, n, d)


def _moe_kernel(ie_ref, it_ref, ilo_ref, ihi_ref, ifirst_ref, ivalid_ref, iflush_ref,
                idx0_ref, idx_next_ref, dst_prev_ref, h_hbm, wg_ref, wu_ref, wd_ref, y_hbm,
                xg, ys, xb, acc, gsem, ssem, *, tm, sub, rps, nf):
    w = pl.program_id(0)
    j = pl.program_id(1)
    tile = it_ref[w]
    slot = tile % 2
    oslot = 1 - slot

    def gather_copy(ids_ref, dslot, r):
        src = h_hbm.at[pl.ds(pl.multiple_of(ids_ref[0, 0, r] * sub, sub), sub), :]
        dst = xg.at[dslot, pl.ds(pl.multiple_of(r * sub, sub), sub), :]
        return pltpu.make_async_copy(src, dst, gsem.at[dslot])

    def scatter_copy(r):
        src = ys.at[oslot, pl.ds(pl.multiple_of(r * sub, sub), sub), :]
        dst = y_hbm.at[pl.ds(pl.multiple_of(dst_prev_ref[0, 0, r] * sub, sub), sub), :]
        return pltpu.make_async_copy(src, dst, ssem.at[oslot])

    def wait_gather(dslot):
        pltpu.make_async_copy(h_hbm.at[pl.ds(0, tm * sub), :], xg.at[dslot], gsem.at[dslot]).wait()

    def wait_scatter():
        pltpu.make_async_copy(ys.at[oslot], y_hbm.at[pl.ds(0, tm * sub), :], ssem.at[oslot]).wait()

    @pl.when(j == 0)
    def _():
        @pl.when(w == 0)
        def _():
            def issue(r, carry):
                gather_copy(idx0_ref, 0, r).start()
                return carry
            lax.fori_loop(0, tm, issue, 0)
            ys[...] = jnp.zeros_like(ys)
            wait_gather(0)

        @pl.when(jnp.logical_and(w > 0, ivalid_ref[jnp.maximum(w - 1, 0)] > 0))
        def _():
            wait_gather((it_ref[jnp.maximum(w - 1, 0)] + 1) % 2)

        @pl.when(ifirst_ref[w] > 0)
        def _():
            for s in range(sub):
                xb[:, s * LANES:(s + 1) * LANES] = xg[slot, pl.ds(s, tm, stride=sub), :].astype(BF16)

        acc[...] = jnp.zeros_like(acc)

        @pl.when(iflush_ref[w] > 0)
        def _():
            def issue(r, carry):
                scatter_copy(r).start()
                return carry
            lax.fori_loop(0, tm, issue, 0)
            wait_scatter()

    @pl.when(ivalid_ref[w] > 0)
    def _():
        base = j * rps
        for k in range(rps):
            gather_copy(idx_next_ref, oslot, base + k).start()
            scatter_copy(base + k).start()
        h = xb[...]
        g = jnp.dot(h, wg_ref[0], preferred_element_type=F32)
        u = jnp.dot(h, wu_ref[0], preferred_element_type=F32)
        acc[...] += jnp.dot((_silu(g) * u).astype(BF16), wd_ref[0], preferred_element_type=F32)

        @pl.when(j == nf - 1)
        def _():
            for r in range(rps * nf, tm):
                gather_copy(idx_next_ref, oslot, r).start()
                scatter_copy(r).start()
            wait_scatter()
            whole = jnp.logical_and(ilo_ref[w] == 0, ihi_ref[w] == tm)

            @pl.when(whole)
            def _():
                for s in range(sub):
                    ys[slot, pl.ds(s, tm, stride=sub), :] = acc[:, s * LANES:(s + 1) * LANES]

            @pl.when(jnp.logical_not(whole))
            def _():
                rowi = lax.broadcasted_iota(jnp.int32, (tm, LANES), 0)
                mine = jnp.logical_and(rowi >= ilo_ref[w], rowi < ihi_ref[w])
                for s in range(sub):
                    old = ys[slot, pl.ds(s, tm, stride=sub), :]
                    ys[slot, pl.ds(s, tm, stride=sub), :] = jnp.where(
                        mine, acc[:, s * LANES:(s + 1) * LANES], old)


def _combine_kernel(y_ref, route_ref, x_ref, mod_ref, post_ref, o_ref, *, tm, sub):
    rt = route_ref[...]
    stride = TOP_K * sub
    y1 = jnp.concatenate([y_ref[pl.ds(s, tm, stride=stride), :] for s in range(sub)], axis=-1)
    y2 = jnp.concatenate([y_ref[pl.ds(sub + s, tm, stride=stride), :] for s in range(sub)], axis=-1)
    y = rt[:, 2:3] * y1 + rt[:, 3:4] * y2
    m = mod_ref[0]
    o_ref[...] = x_ref[...] + m[5:6] * _rms(y, post_ref[...])


def _ffn_moe(hrows, route, xs, mod_l, mod_row, post_gain, wg, wu, wd):
    b, n, d = xs.shape
    r = b * n
    n_pairs = TOP_K * r
    n_exp, _, f = wg.shape
    sub = d // LANES
    tm = _pick_tile(n_pairs, 1024, SUBLANES)
    tf = _pick_tile(f, 512, LANES)
    nf = f // tf
    rps = tm // nf
    n_tiles = n_pairs // tm
    n_items = n_tiles + n_exp

    e_flat = route[:, :TOP_K].astype(jnp.int32).reshape(-1)
    pair_ids = jnp.arange(n_pairs, dtype=jnp.int32)
    order = jnp.sort(e_flat * n_pairs + pair_ids) % n_pairs
    counts = jnp.sum((e_flat[:, None] == jnp.arange(n_exp, dtype=jnp.int32)[None, :]).astype(jnp.int32),
                     axis=0)
    cend = jnp.cumsum(counts)
    cstart = cend - counts
    first_tile = cstart // tm
    last_tile = jnp.maximum(cend - 1, 0) // tm
    n_e = jnp.where(counts > 0, last_tile - first_tile + 1, 0)
    item_end = jnp.cumsum(n_e)
    item_start = item_end - n_e
    total = item_end[-1]
    wi = jnp.arange(n_items, dtype=jnp.int32)
    item_e = jnp.minimum(jnp.sum((wi[:, None] >= item_end[None, :]).astype(jnp.int32), axis=1), n_exp - 1)
    item_valid = (wi < total).astype(jnp.int32)
    item_tile = jnp.where(item_valid > 0, first_tile[item_e] + wi - item_start[item_e], n_tiles)
    item_lo = jnp.clip(cstart[item_e] - item_tile * tm, 0, tm) * item_valid
    item_hi = jnp.clip(cend[item_e] - item_tile * tm, 0, tm) * item_valid
    prev_tile = jnp.concatenate([jnp.full((1,), -1, jnp.int32), item_tile[:-1]])
    item_first = jnp.logical_and(item_valid > 0, item_tile != prev_tile).astype(jnp.int32)
    item_flush = (wi == total).astype(jnp.int32)
    last_e = item_e[jnp.maximum(total - 1, 0)]
    item_e = jnp.where(item_valid > 0, item_e, last_e)

    src3 = (order // TOP_K).reshape(n_tiles, 1, tm)
    dst3 = order.reshape(n_tiles, 1, tm)

    def w_col(w, j, ie, it, ilo, ihi, ifi, iva, ifl):
        return (ie[w], 0, jnp.where(iva[w] > 0, j, nf - 1))

    def w_row(w, j, ie, it, ilo, ihi, ifi, iva, ifl):
        return (ie[w], jnp.where(iva[w] > 0, j, nf - 1), 0)

    smem = pltpu.MemorySpace.SMEM
    y2 = pl.pallas_call(
        functools.partial(_moe_kernel, tm=tm, sub=sub, rps=rps, nf=nf),
        grid_spec=pltpu.PrefetchScalarGridSpec(
            num_scalar_prefetch=7,
            grid=(n_items, nf),
            in_specs=[
                pl.BlockSpec((1, 1, tm), lambda w, j, *_: (0, 0, 0), memory_space=smem),
                pl.BlockSpec((1, 1, tm), lambda w, j, ie, it, *_: (jnp.minimum(it[w] + 1, n_tiles - 1), 0, 0),
                             memory_space=smem),
                pl.BlockSpec((1, 1, tm), lambda w, j, ie, it, *_: (jnp.clip(it[w] - 1, 0, n_tiles - 1), 0, 0),
                             memory_space=smem),
                pl.BlockSpec(memory_space=pl.ANY),
                pl.BlockSpec((1, d, tf), w_col),
                pl.BlockSpec((1, d, tf), w_col),
                pl.BlockSpec((1, tf, d), w_row),
            ],
            out_specs=pl.BlockSpec(memory_space=pl.ANY),
            scratch_shapes=[
                pltpu.VMEM((2, tm * sub, LANES), F32),
                pltpu.VMEM((2, tm * sub, LANES), F32),
                pltpu.VMEM((tm, d), BF16),
                pltpu.VMEM((tm, d), F32),
                pltpu.SemaphoreType.DMA((2,)),
                pltpu.SemaphoreType.DMA((2,)),
            ],
        ),
        out_shape=jax.ShapeDtypeStruct((n_pairs * sub, LANES), F32),
        compiler_params=_cparams(("arbitrary", "arbitrary")),
        name="moe_experts",
    )(item_e, item_tile, item_lo, item_hi, item_first, item_valid, item_flush,
      src3, src3, dst3, hrows, wg, wu, wd)

    tmc = _pick_tile(n, 512, SUBLANES)
    tiles_per_batch = n // tmc
    if mod_row is None:
        mod_map = lambda i: (i // tiles_per_batch, 0, 0)
    else:
        mod_map = lambda i: (mod_row, 0, 0)
    out = pl.pallas_call(
        functools.partial(_combine_kernel, tm=tmc, sub=sub),
        grid=(r // tmc,),
        in_specs=[
            pl.BlockSpec((tmc * TOP_K * sub, LANES), lambda i: (i, 0)),
            pl.BlockSpec((tmc, LANES), lambda i: (i, 0)),
            pl.BlockSpec((tmc, d), lambda i: (i, 0)),
            pl.BlockSpec((1, 6, d), mod_map),
            pl.BlockSpec((1, d), lambda i: (0, 0)),
        ],
        out_specs=pl.BlockSpec((tmc, d), lambda i: (i, 0)),
        out_shape=jax.ShapeDtypeStruct((r, d), F32),
        compiler_params=_cparams(("parallel",)),
        name="moe_combine",
    )(y2, route, xs.reshape(r, d), mod_l, post_gain.reshape(1, d))
    return out.reshape(b, n, d)


def kernel(x, c, ctx, c_ctx, ada_w, ada_b, mix_pre, mix_post, ffn_pre, ffn_post, w_in, hgrn_lb, hgrn_gain,
           pool_w, pool_scale, w_branch_rec, w_branch_pool, w_out, ffn_w_gate, ffn_w_up, ffn_w_down,
           moe_router, moe_w_gate, moe_w_up, moe_w_down):
    b, n, d = x.shape
    depth = ada_w.shape[0]
    assert b + 1 <= MOD_ROWS and d % LANES == 0
    ctx_row = b

    cond = jnp.zeros((MOD_ROWS, d), F32).at[:b].set(c).at[b].set(c_ctx)
    mod = _modulation(cond, ada_w, ada_b).reshape(depth, MOD_ROWS, 6, d)

    lb_all = jnp.cumsum(jax.nn.softmax(hgrn_lb.astype(F32), axis=0), axis=0)
    lb_all = lb_all - lb_all[0:1]

    bf = lambda a: a.astype(BF16)
    w_in_b, pool_w_b = bf(w_in), bf(pool_w)
    w_rec_b, w_pool_b, w_out_b = bf(w_branch_rec), bf(w_branch_pool), bf(w_out)
    ffn_g, ffn_u, ffn_d = bf(ffn_w_gate), bf(ffn_w_up), bf(ffn_w_down)
    moe_g, moe_u, moe_d = bf(moe_w_gate), bf(moe_w_up), bf(moe_w_down)

    u_blk = (5 * D_HGRN) // D_POOL
    s_zero = jnp.zeros((b, HEADS, HEAD_DIM, HEAD_DIM), F32)

    def mixer(xs, l, mod_row, s0f, s0b, on_grid, router):
        p = _in_proj(xs, mod[l], mod_row, mix_pre[l], w_in_b[l])
        o, s_f, s_b = _gla(p, lb_all[l], s0f, s0b, with_out=True, col0=0)
        yp = _pool(p, pool_w_b[l], pool_scale[l], on_grid=on_grid, col_block=u_blk)
        outs = _merge(o, p, yp, xs, mod[l], mod_row, hgrn_gain[l], mix_post[l], ffn_pre[l],
                      w_rec_b[l], w_pool_b[l], w_out_b[l], router=router)
        return outs, s_f, s_b

    def channel(outs, l, mod_row):
        j = l // 2
        if l % 2 == 0:
            xn, h2 = outs
            return _ffn_dense(h2, xn, mod[l], mod_row, ffn_post[l], ffn_g[j], ffn_u[j], ffn_d[j])
        xn, hrows, route = outs
        return _ffn_moe(hrows, route, xn, mod[l], mod_row, ffn_post[l], moe_g[j], moe_u[j], moe_d[j])

    cx = ctx
    for l in range(depth):
        last = l == depth - 1
        router = moe_router[l // 2] if l % 2 == 1 else None
        if last:
            pc = _in_proj(cx, mod[l], ctx_row, mix_pre[l], w_in_b[l][:, D_HGRN:4 * D_HGRN])
            s_f, s_b = _gla(pc, lb_all[l], s_zero, s_zero, with_out=False, col0=0)
            c_outs = None
        else:
            c_outs, s_f, s_b = mixer(cx, l, ctx_row, s_zero, s_zero, False, router)
        x_outs, _, _ = mixer(x, l, None, s_f, s_b, True, router)
        x = channel(x_outs, l, None)
        if not last:
            cx = channel(c_outs, l, ctx_row)
    return x
```

```python
import functools

import jax
import jax.numpy as jnp
from jax import lax
from jax.experimental import pallas as pl
from jax.experimental.pallas import tpu as pltpu

F32 = jnp.float32
BF16 = jnp.bfloat16

EPS = 1e-6
LANES = 128
SUBLANES = 8
HEADS = 4
HEAD_DIM = 128
D_HGRN = HEADS * HEAD_DIM
D_POOL = 512
POOL_WINDOWS = (2, 4, 8, 16)
POOL_GROUP = D_POOL // len(POOL_WINDOWS)
GRID_W = 64
CHUNK = 64
SUBCHUNK = CHUNK // 2
SAFE_DECAY = 85.0
DIAG = SUBLANES
GLA_HEADS_PER_STEP = 2
STATE_UNROLL = 4
ATT_UNROLL = 2
TOP_K = 2
MOD_ROWS = 40
VMEM_LIMIT = 56 * 1024 * 1024

def _cparams(sem):
    return pltpu.CompilerParams(dimension_semantics=sem, vmem_limit_bytes=VMEM_LIMIT)


def _pick_tile(n, cap, mult):
    if n <= cap:
        return n
    best = None
    for t in range(mult, cap + 1, mult):
        if n % t == 0:
            best = t
    assert best is not None, (n, cap, mult)
    return best


def _sigmoid(z):
    return 1.0 / (1.0 + jnp.exp(-z))


def _silu(z):
    return z * _sigmoid(z)


def _mod_kernel(c_ref, w_ref, b_ref, o_ref):
    a = _silu(c_ref[...])
    o_ref[0] = jnp.dot(a, w_ref[0], preferred_element_type=F32,
                       precision=lax.Precision.HIGHEST) + b_ref[0]


def _modulation(cond, ada_w, ada_b):
    depth, d, six_d = ada_w.shape
    tn = _pick_tile(six_d, 1024, LANES)
    return pl.pallas_call(
        _mod_kernel,
        grid=(depth, six_d // tn),
        in_specs=[
            pl.BlockSpec((MOD_ROWS, d), lambda l, j: (0, 0)),
            pl.BlockSpec((1, d, tn), lambda l, j: (l, 0, j)),
            pl.BlockSpec((1, 1, tn), lambda l, j: (l, 0, j)),
        ],
        out_specs=pl.BlockSpec((1, MOD_ROWS, tn), lambda l, j: (l, 0, j)),
        out_shape=jax.ShapeDtypeStruct((depth, MOD_ROWS, six_d), F32),
        compiler_params=_cparams(("parallel", "parallel")),
        name="modulation",
    )(cond, ada_w, ada_b.reshape(depth, 1, six_d))


def _rms(x, gain):
    ms = jnp.mean(x * x, axis=-1, keepdims=True)
    return x * lax.rsqrt(ms + EPS) * gain


def _inproj_kernel(x_ref, mod_ref, gain_ref, w_ref, o_ref, *, tn):
    m = mod_ref[0]
    h = _rms(x_ref[0], gain_ref[...]) * (1.0 + m[1:2]) + m[0:1]
    hb = h.astype(BF16)
    for j in range(w_ref.shape[1] // tn):
        o_ref[0, :, j * tn:(j + 1) * tn] = jnp.dot(
            hb, w_ref[:, j * tn:(j + 1) * tn], preferred_element_type=F32).astype(o_ref.dtype)


def _in_proj(xs, mod_l, mod_row, gain, w):
    b, n, d = xs.shape
    nc = w.shape[1]
    tm = _pick_tile(n, 512, SUBLANES)
    tn = _pick_tile(nc, 512, LANES)
    if mod_row is None:
        mod_map = lambda bi, i: (bi, 0, 0)
    else:
        mod_map = lambda bi, i: (mod_row, 0, 0)
    return pl.pallas_call(
        functools.partial(_inproj_kernel, tn=tn),
        grid=(b, n // tm),
        in_specs=[
            pl.BlockSpec((1, tm, d), lambda bi, i: (bi, i, 0)),
            pl.BlockSpec((1, 6, d), mod_map),
            pl.BlockSpec((1, d), lambda bi, i: (0, 0)),
            pl.BlockSpec((d, nc), lambda bi, i: (0, 0)),
        ],
        out_specs=pl.BlockSpec((1, tm, nc), lambda bi, i: (bi, i, 0)),
        out_shape=jax.ShapeDtypeStruct((b, n, nc), BF16),
        compiler_params=_cparams(("parallel", "parallel")),
        name="in_proj",
    )(xs, mod_l, gain.reshape(1, d), w)


def _neg_abs(a):
    bits = lax.bitcast_convert_type(a, jnp.uint32) | jnp.uint32(0x80000000)
    return lax.bitcast_convert_type(bits, F32)


def _dot_nt(a, b):
    return lax.dot_general(a, b, (((1,), (1,)), ((), ())), preferred_element_type=F32)


def _dot_tn(a, b):
    return lax.dot_general(a, b, (((0,), (0,)), ((), ())), preferred_element_type=F32)


def _gla_kernel(*refs, n, with_out, hpb):
    if with_out:
        (q_ref, ff_ref, fb_ref, i_ref, lb_ref, s0f_ref, s0b_ref,
         o_ref, sf_ref, sb_ref, qs, qf_f, qf_b, vs, cum_f, cum_b, k_f, k_b, kh_f, kh_b) = refs
    else:
        (ff_ref, fb_ref, i_ref, lb_ref, s0f_ref, s0b_ref,
         sf_ref, sb_ref, vs, cum_f, cum_b, k_f, k_b, kh_f, kh_b) = refs
    c = CHUNK
    nchunks = n // c
    nblk = c // DIAG

    hb = SUBCHUNK
    pos_c = lax.broadcasted_iota(jnp.int32, (c, hpb * HEAD_DIM), 0) % hb

    def prepare(ci, low):
        rows = pl.ds(pl.multiple_of(ci * c, c), c)
        if with_out:
            qs[rows, :] = _silu(q_ref[0, rows, :].astype(F32)) * (HEAD_DIM ** -0.5)
        vs[rows, :] = i_ref[0, rows, :].astype(F32)
        for rev in (False, True):
            z = (fb_ref if rev else ff_ref)[0, rows, :].astype(F32)
            lb = lb_ref[1:2, :] if rev else lb_ref[0:1, :]
            e = jnp.exp(-jnp.abs(z))
            log_sig = jnp.minimum(z, 0.0) - jnp.log(1.0 + e)
            a = jnp.log(lb)
            bt = jnp.log(1.0 - lb) + log_sig
            mx = jnp.maximum(a, bt)
            mn = jnp.minimum(a, bt)
            s = mx + jnp.log(1.0 + jnp.exp(mn - mx))
            (k_b if rev else k_f)[rows, :] = (1.0 - lb) * (jnp.where(z >= 0.0, e, 1.0) / (1.0 + e))
            sh = 1
            while sh < hb:
                if rev:
                    s = s + jnp.where(pos_c < hb - sh, pltpu.roll(s, c - sh, 0), 0.0)
                else:
                    s = s + jnp.where(pos_c >= sh, pltpu.roll(s, sh, 0), 0.0)
                sh *= 2
            (cum_b if rev else cum_f)[rows, :] = s
            low = jnp.minimum(low, s)
        return low

    lowest = jnp.min(lax.fori_loop(0, nchunks, prepare, jnp.zeros((c, hpb * HEAD_DIM), F32)))

    row = lax.broadcasted_iota(jnp.int32, (c, c), 0)
    col = lax.broadcasted_iota(jnp.int32, (c, c), 1)
    rowv = lax.broadcasted_iota(jnp.int32, (c, HEAD_DIM), 0)
    row3 = lax.broadcasted_iota(jnp.int32, (nblk, DIAG, HEAD_DIM), 1)

    def level_map(rev):
        lv = jnp.zeros((c, c), jnp.int32)
        m = c
        while m > 1:
            half = m // 2
            t_right = (row % m) >= half
            s_right = (col % m) >= half
            pair = jnp.logical_and(jnp.logical_not(t_right), s_right) if rev else \
                jnp.logical_and(t_right, jnp.logical_not(s_right))
            lv = jnp.where(jnp.logical_and((row // m) == (col // m), pair), m, lv)
            m = half
        return lv

    lv_f, lv_b = level_map(False), level_map(True)
    t_rows_of = {}
    m = c
    while m > 1:
        in_right = (rowv % m) >= m // 2
        t_rows_of[(False, m)] = in_right
        t_rows_of[(True, m)] = jnp.logical_not(in_right)
        m //= 2

    far_of = {False: rowv >= hb, True: rowv < hb}

    def one_chunk(rev, cidx, st, accumulate, hh):
        r0 = pl.multiple_of(cidx * c, c)
        hs = slice(hh * HEAD_DIM, (hh + 1) * HEAD_DIM)
        loc = (cum_b if rev else cum_f)[pl.ds(r0, c), hs]
        kc = (k_b if rev else k_f)[pl.ds(r0, c), hs]
        vc = vs[pl.ds(r0, c), hs]
        vcb = vc.astype(BF16)
        far = far_of[rev]
        first = loc[hb:hb + 1, :] if rev else loc[hb - 1:hb, :]
        end = loc[0:1, :] if rev else loc[c - 1:c, :]
        tot = first + end

        def put(o):
            if accumulate:
                o_ref[0, pl.ds(r0, c), hs] = o_ref[0, pl.ds(r0, c), hs] + o
            else:
                o_ref[0, pl.ds(r0, c), hs] = o

        cum = loc + jnp.where(far, first, 0.0)
        if with_out:
            qc = qs[pl.ds(r0, c), hs]
            o = _dot_nt((qc * jnp.exp(cum)).astype(BF16), st.astype(BF16))
            lv = lv_b if rev else lv_f
            c3 = cum.reshape(nblk, DIAG, HEAD_DIM)
            att = jnp.zeros((c, c), F32)
            m = c
            while m > 1:
                half = m // 2
                t_rows = t_rows_of[(rev, m)]
                if m > DIAG:
                    pieces = []
                    for blk in range(c // m):
                        rr = blk * m + (half if rev else half - 1)
                        pieces.append(jnp.broadcast_to(cum[rr:rr + 1, :], (m, HEAD_DIM)))
                    ref = pieces[0] if len(pieces) == 1 else jnp.concatenate(pieces, axis=0)
                elif m > 2:
                    ref3 = None
                    for grp in reversed(range(DIAG // m)):
                        rr = grp * m + (half if rev else half - 1)
                        piece = jnp.broadcast_to(c3[:, rr:rr + 1, :], (nblk, DIAG, HEAD_DIM))
                        ref3 = piece if ref3 is None else jnp.where(row3 < (grp + 1) * m, piece, ref3)
                    ref = ref3.reshape(c, HEAD_DIM)
                else:
                    ref = jnp.where(t_rows, pltpu.roll(cum, (c - 1) if rev else 1, 0), cum)
                w = jnp.where(t_rows, qc, kc) * jnp.exp(_neg_abs(cum - ref))
                wb = w.astype(BF16)
                att = jnp.where(lv == m, _dot_nt(wb, wb), att)
                m = half
            o = o + jnp.dot(att.astype(BF16), vcb, preferred_element_type=F32)
            put(o + jnp.sum(qc * kc, axis=-1, keepdims=True) * vc)

        khat = kc * jnp.exp(tot - cum)
        return jnp.exp(tot) * st + _dot_tn(vcb, khat.astype(BF16))

    def make_body(accumulate):
        def body(ci, carry):
            st_f = tuple(one_chunk(False, ci, carry[hh], accumulate, hh) for hh in range(hpb))
            st_b = tuple(one_chunk(True, nchunks - 1 - ci, carry[hpb + hh], accumulate, hh)
                         for hh in range(hpb))
            return st_f + st_b
        return body

    def initial_states():
        return tuple(s0f_ref[0, hh] for hh in range(hpb)) + tuple(s0b_ref[0, hh] for hh in range(hpb))

    def put_states(carry):
        for hh in range(hpb):
            sf_ref[0, hh] = carry[hh]
            sb_ref[0, hh] = carry[hpb + hh]

    def sweep_midpoint():
        carry = lax.fori_loop(0, nchunks // 2, make_body(False), initial_states())
        put_states(lax.fori_loop(nchunks // 2, nchunks, make_body(True), carry))

    ab = 2 * c
    rowa = lax.broadcasted_iota(jnp.int32, (ab, ab), 0)
    cola = lax.broadcasted_iota(jnp.int32, (ab, ab), 1)
    rowb = lax.broadcasted_iota(jnp.int32, (ab, HEAD_DIM), 0)

    same_chunk = (rowa // c) == (cola // c)
    keep_of = {False: jnp.logical_and(same_chunk, cola <= rowa),
               True: jnp.logical_and(same_chunk, cola >= rowa)}
    far_b = {False: (rowb % c) >= hb, True: (rowb % c) < hb}

    def chunk_rows(x, r):
        return jnp.concatenate(
            [jnp.broadcast_to(x[ch * c + r:ch * c + r + 1, :], (c, HEAD_DIM)) for ch in range(ab // c)],
            axis=0)

    def att_block(rev, r0, hh):
        hs = slice(hh * HEAD_DIM, (hh + 1) * HEAD_DIM)
        loc = (cum_b if rev else cum_f)[pl.ds(r0, ab), hs]
        far = far_b[rev]
        e_first = jnp.exp(chunk_rows(loc, hb if rev else hb - 1))
        e_end = jnp.exp(chunk_rows(loc, 0 if rev else c - 1))
        k_loc = (k_b if rev else k_f)[pl.ds(r0, ab), hs] * jnp.exp(-loc)
        (kh_b if rev else kh_f)[pl.ds(r0, ab), hs] = (
            k_loc * jnp.where(far, e_end, e_end * e_first)).astype(BF16)
        if not with_out:
            return None
        q_loc = qs[pl.ds(r0, ab), hs] * jnp.exp(loc)
        (qf_b if rev else qf_f)[pl.ds(r0, ab), hs] = (q_loc * jnp.where(far, e_first, 1.0)).astype(BF16)
        q2 = jnp.concatenate([jnp.where(far, 0.0, q_loc), jnp.where(far, q_loc, 0.0)], axis=1)
        k2 = jnp.concatenate([k_loc, jnp.where(far, k_loc, k_loc * e_first)], axis=1)
        att = jnp.where(keep_of[rev], _dot_nt(q2.astype(BF16), k2.astype(BF16)), 0.0)
        return jnp.dot(att.astype(BF16), vs[pl.ds(r0, ab), hs].astype(BF16), preferred_element_type=F32)

    def att_body(gi, carry):
        for u in range(ATT_UNROLL):
            r0 = pl.multiple_of((gi * ATT_UNROLL + u) * ab, ab)
            for hh in range(hpb):
                o_f = att_block(False, r0, hh)
                o_b = att_block(True, r0, hh)
                if with_out:
                    o_ref[0, pl.ds(r0, ab), hh * HEAD_DIM:(hh + 1) * HEAD_DIM] = o_f + o_b
        return carry

    def state_chunk(rev, cidx, st, hh):
        r0 = pl.multiple_of(cidx * c, c)
        hs = slice(hh * HEAD_DIM, (hh + 1) * HEAD_DIM)
        sums_ref = cum_b if rev else cum_f

        def row_at(off):
            grp = off // SUBLANES * SUBLANES
            blk = sums_ref[pl.ds(pl.multiple_of(r0 + grp, SUBLANES), SUBLANES), hs]
            return blk[off - grp:off - grp + 1, :]

        first = row_at(hb if rev else hb - 1)
        end = row_at(0 if rev else c - 1)
        if with_out:
            qf = (qf_b if rev else qf_f)[pl.ds(r0, c), hs]
            (k_b if rev else k_f)[pl.ds(r0, c), hs] = _dot_nt(qf, st.astype(BF16))
        kh = (kh_b if rev else kh_f)[pl.ds(r0, c), hs]
        return jnp.exp(first + end) * st + _dot_tn(vs[pl.ds(r0, c), hs].astype(BF16), kh)

    def state_body(gi, carry):
        for u in range(STATE_UNROLL):
            ci = gi * STATE_UNROLL + u
            st_f = tuple(state_chunk(False, ci, carry[hh], hh) for hh in range(hpb))
            st_b = tuple(state_chunk(True, nchunks - 1 - ci, carry[hpb + hh], hh) for hh in range(hpb))
            carry = st_f + st_b
        return carry

    def add_state_part(ci, carry):
        rows = pl.ds(pl.multiple_of(ci * c, c), c)
        o_ref[0, rows, :] = o_ref[0, rows, :] + k_f[rows, :] + k_b[rows, :]
        return carry

    def sweep_one_sided():
        lax.fori_loop(0, n // (ab * ATT_UNROLL), att_body, 0)
        put_states(lax.fori_loop(0, nchunks // STATE_UNROLL, state_body, initial_states()))
        if with_out:
            lax.fori_loop(0, nchunks, add_state_part, 0)

    in_range = lowest >= -SAFE_DECAY

    @pl.when(in_range)
    def _():
        sweep_one_sided()

    @pl.when(jnp.logical_not(in_range))
    def _():
        sweep_midpoint()


def _gla(p, lb, s0f, s0b, *, with_out, col0):
    b, n, _ = p.shape
    assert n % (2 * CHUNK) == 0
    assert n % (STATE_UNROLL * CHUNK) == 0 and n % (ATT_UNROLL * 2 * CHUNK) == 0
    hpb = GLA_HEADS_PER_STEP
    groups = HEADS // hpb
    wblk = hpb * HEAD_DIM
    assert col0 % hpb == 0

    def col_spec(k):
        return pl.BlockSpec((1, n, wblk), lambda bi, h, k=k: (bi, 0, col0 // hpb + k * groups + h))

    st_spec = pl.BlockSpec((1, hpb, HEAD_DIM, HEAD_DIM), lambda bi, h: (bi, h, 0, 0))
    n_cols = 4 if with_out else 3
    in_specs = [col_spec(k) for k in range(n_cols)] + [
        pl.BlockSpec((2, wblk), lambda bi, h: (0, h)), st_spec, st_spec]
    st_shape = jax.ShapeDtypeStruct((b, HEADS, HEAD_DIM, HEAD_DIM), F32)
    out_specs = [st_spec, st_spec]
    out_shape = [st_shape, st_shape]
    scratch = [pltpu.VMEM((n, wblk), F32) for _ in range(5)] + [pltpu.VMEM((n, wblk), BF16)] * 2
    if with_out:
        out_specs = [pl.BlockSpec((1, n, wblk), lambda bi, h: (bi, 0, h))] + out_specs
        out_shape = [jax.ShapeDtypeStruct((b, n, D_HGRN), F32)] + out_shape
        scratch = [pltpu.VMEM((n, wblk), F32)] + [pltpu.VMEM((n, wblk), BF16)] * 2 + scratch
    return pl.pallas_call(
        functools.partial(_gla_kernel, n=n, with_out=with_out, hpb=hpb),
        grid=(b, groups),
        in_specs=in_specs,
        out_specs=out_specs,
        out_shape=out_shape,
        scratch_shapes=scratch,
        compiler_params=_cparams(("parallel", "parallel")),
        name="gla" if with_out else "gla_state",
    )(*([p] * n_cols), lb, s0f, s0b)


def _pool_kernel(u_ref, pw_ref, ps_ref, o_ref, *, n, width):
    rows = n // width
    t = lax.broadcasted_iota(jnp.int32, (n, POOL_GROUP), 0)
    colp = t % width
    rowp = t // width

    def shift(a, delta, pos, length, stride):
        rolled = pltpu.roll(a, (delta * stride) % n, 0)
        ok = jnp.logical_and(pos - delta >= 0, pos - delta < length)
        return jnp.where(ok, rolled, 0.0)

    def box_sum(a, w, pos, length, stride):
        trail, lead = a, a
        h = 1
        while h < w // 2:
            trail = trail + shift(trail, h, pos, length, stride)
            lead = lead + shift(lead, -h, pos, length, stride)
            h *= 2
        return shift(trail, 1, pos, length, stride) + lead

    def count(pos, w, length):
        lo = jnp.clip(pos - w // 2, 0, length)
        hi = jnp.clip(pos - w // 2 + w, 0, length)
        return (hi - lo).astype(F32)

    for gi, w in enumerate(POOL_WINDOWS):
        ug = u_ref[0, :, gi * POOL_GROUP:(gi + 1) * POOL_GROUP].astype(F32)
        if rows > 1:
            s = box_sum(ug, w, rowp, rows, width) / count(rowp, w, rows)
            s = box_sum(s, w, colp, width, 1) / count(colp, w, width)
        else:
            s = box_sum(ug, w, colp, width, 1) / count(colp, w, width)
        y = jnp.dot((s - ug).astype(BF16), pw_ref[gi], preferred_element_type=F32)
        y = y * ps_ref[:, gi * POOL_GROUP:(gi + 1) * POOL_GROUP]
        o_ref[0, :, gi * POOL_GROUP:(gi + 1) * POOL_GROUP] = y.astype(o_ref.dtype)


def _pool(p, pool_w, pool_scale, *, on_grid, col_block):
    b, n, _ = p.shape
    width = GRID_W if on_grid else n
    assert n % width == 0 and width & (width - 1) == 0
    return pl.pallas_call(
        functools.partial(_pool_kernel, n=n, width=width),
        grid=(b,),
        in_specs=[
            pl.BlockSpec((1, n, D_POOL), lambda bi: (bi, 0, col_block)),
            pl.BlockSpec(pool_w.shape, lambda bi: (0, 0, 0)),
            pl.BlockSpec((1, D_POOL), lambda bi: (0, 0)),
        ],
        out_specs=pl.BlockSpec((1, n, D_POOL), lambda bi: (bi, 0, 0)),
        out_shape=jax.ShapeDtypeStruct((b, n, D_POOL), BF16),
        compiler_params=_cparams(("parallel",)),
        name="pool",
    )(p, pool_w, pool_scale.reshape(1, D_POOL))


def _merge_kernel(*refs, moe, n_experts):
    (o_ref, g_ref, yp_ref, ga_ref, gb_ref, x_ref, mod_ref, og_ref, post_ref, pre2_ref,
     wr_ref, wp_ref, wo_ref) = refs[:13]
    if moe:
        router_ref, xn_ref, hrow_ref, route_ref = refs[13:]
    else:
        xn_ref, h2_ref = refs[13:]
    tm = x_ref.shape[1]
    m = mod_ref[0]
    o = o_ref[0]
    parts = []
    for h in range(HEADS):
        oh = o[:, h * HEAD_DIM:(h + 1) * HEAD_DIM]
        parts.append(oh * lax.rsqrt(jnp.mean(oh * oh, axis=-1, keepdims=True) + EPS))
    on = jnp.concatenate(parts, axis=-1) * og_ref[...]
    y_rec = (on * _silu(g_ref[0].astype(F32))).astype(BF16)
    rec = jnp.dot(y_rec, wr_ref[...], preferred_element_type=F32)
    pool = jnp.dot(yp_ref[0], wp_ref[...], preferred_element_type=F32)
    merged = _sigmoid(ga_ref[0].astype(F32)) * rec + _sigmoid(gb_ref[0].astype(F32)) * pool
    y = jnp.dot(merged.astype(BF16), wo_ref[...], preferred_element_type=F32)
    xn = x_ref[0] + m[2:3] * _rms(y, post_ref[...])
    xn_ref[0] = xn
    h2 = _rms(xn, pre2_ref[...]) * (1.0 + m[4:5]) + m[3:4]
    if not moe:
        h2_ref[0] = h2.astype(h2_ref.dtype)
        return
    d = h2.shape[-1]
    for s in range(d // LANES):
        hrow_ref[pl.ds(s, tm, stride=d // LANES), :] = h2[:, s * LANES:(s + 1) * LANES]
    h_hi = h2.astype(BF16)
    h_lo = (h2 - h_hi.astype(F32)).astype(BF16)
    l2 = jnp.dot(h_hi, router_ref[...], preferred_element_type=F32)
    logits = (l2[:, :LANES] + l2[:, LANES:]
              + jnp.dot(h_lo, router_ref[:, :LANES], preferred_element_type=F32))
    lane = lax.broadcasted_iota(jnp.int32, logits.shape, 1)
    neg = jnp.float32(-jnp.inf)
    lg = jnp.where(lane < n_experts, logits, neg)
    m1 = jnp.max(lg, axis=-1, keepdims=True)
    i1 = jnp.min(jnp.where(lg == m1, lane, LANES), axis=-1, keepdims=True)
    lg2 = jnp.where(lane == i1, neg, lg)
    m2 = jnp.max(lg2, axis=-1, keepdims=True)
    i2 = jnp.min(jnp.where(lg2 == m2, lane, LANES), axis=-1, keepdims=True)
    e2 = jnp.exp(m2 - m1)
    p1 = 1.0 / (1.0 + e2)
    p2 = e2 * p1
    route = jnp.where(lane == 0, i1.astype(F32),
                      jnp.where(lane == 1, i2.astype(F32),
                                jnp.where(lane == 2, p1, jnp.where(lane == 3, p2, 0.0))))
    route_ref[...] = route


def _merge(o, p, ypool, xs, mod_l, mod_row, o_gain, post_gain, pre2_gain, w_rec, w_pool, w_out,
           router=None):
    b, n, d = xs.shape
    moe = router is not None
    tm = _pick_tile(n, 512, SUBLANES)
    nt = n // tm
    if mod_row is None:
        mod_map = lambda bi, i: (bi, 0, 0)
    else:
        mod_map = lambda bi, i: (mod_row, 0, 0)
    g_blk = 4 * D_HGRN // D_HGRN
    ga_col = 5 * D_HGRN + D_POOL
    assert ga_col % d == 0
    ga_blk = ga_col // d
    const = lambda bi, i: (0, 0)
    in_specs = [
        pl.BlockSpec((1, tm, D_HGRN), lambda bi, i: (bi, i, 0)),
        pl.BlockSpec((1, tm, D_HGRN), lambda bi, i: (bi, i, g_blk)),
        pl.BlockSpec((1, tm, D_POOL), lambda bi, i: (bi, i, 0)),
        pl.BlockSpec((1, tm, d), lambda bi, i: (bi, i, ga_blk)),
        pl.BlockSpec((1, tm, d), lambda bi, i: (bi, i, ga_blk + 1)),
        pl.BlockSpec((1, tm, d), lambda bi, i: (bi, i, 0)),
        pl.BlockSpec((1, 6, d), mod_map),
        pl.BlockSpec((1, D_HGRN), const),
        pl.BlockSpec((1, d), const),
        pl.BlockSpec((1, d), const),
        pl.BlockSpec((D_HGRN, d), const),
        pl.BlockSpec((D_POOL, d), const),
        pl.BlockSpec((d, d), const),
    ]
    args = [o, p, ypool, p, p, xs, mod_l, o_gain.reshape(1, D_HGRN), post_gain.reshape(1, d),
            pre2_gain.reshape(1, d), w_rec, w_pool, w_out]
    out_specs = [pl.BlockSpec((1, tm, d), lambda bi, i: (bi, i, 0))]
    out_shape = [jax.ShapeDtypeStruct((b, n, d), F32)]
    n_experts = 0
    if moe:
        n_experts = router.shape[1]
        router_pad = jnp.zeros((d, LANES), F32).at[:, :n_experts].set(router)
        r_hi = router_pad.astype(BF16)
        r_lo = (router_pad - r_hi.astype(F32)).astype(BF16)
        in_specs.append(pl.BlockSpec((d, 2 * LANES), const))
        args.append(jnp.concatenate([r_hi, r_lo], axis=1))
        sub = d // LANES
        out_specs += [pl.BlockSpec((tm * sub, LANES), lambda bi, i: (bi * nt + i, 0)),
                      pl.BlockSpec((tm, LANES), lambda bi, i: (bi * nt + i, 0))]
        out_shape += [jax.ShapeDtypeStruct((b * n * sub, LANES), F32),
                      jax.ShapeDtypeStruct((b * n, LANES), F32)]
    else:
        out_specs.append(pl.BlockSpec((1, tm, d), lambda bi, i: (bi, i, 0)))
        out_shape.append(jax.ShapeDtypeStruct((b, n, d), BF16))
    return pl.pallas_call(
        functools.partial(_merge_kernel, moe=moe, n_experts=n_experts),
        grid=(b, nt),
        in_specs=in_specs,
        out_specs=out_specs,
        out_shape=out_shape,
        compiler_params=_cparams(("parallel", "parallel")),
        name="merge_moe" if moe else "merge",
    )(*args)


def _ffn_kernel(h_ref, wg_ref, wu_ref, wd_ref, x_ref, mod_ref, post_ref, o_ref, acc):
    j = pl.program_id(1)

    @pl.when(j == 0)
    def _():
        acc[...] = jnp.zeros_like(acc)

    h = h_ref[...]
    g = jnp.dot(h, wg_ref[...], preferred_element_type=F32)
    u = jnp.dot(h, wu_ref[...], preferred_element_type=F32)
    acc[...] += jnp.dot((_silu(g) * u).astype(BF16), wd_ref[...], preferred_element_type=F32)

    @pl.when(j == pl.num_programs(1) - 1)
    def _():
        m = mod_ref[0]
        o_ref[...] = x_ref[...] + m[5:6] * _rms(acc[...], post_ref[...])


def _ffn_dense(h2, xs, mod_l, mod_row, post_gain, wg, wu, wd):
    b, n, d = xs.shape
    f = wg.shape[1]
    r = b * n
    tm = _pick_tile(n, 512, SUBLANES)
    tf = _pick_tile(f, 1408, LANES)
    tiles_per_batch = n // tm
    if mod_row is None:
        mod_map = lambda i, j: (i // tiles_per_batch, 0, 0)
    else:
        mod_map = lambda i, j: (mod_row, 0, 0)
    out = pl.pallas_call(
        _ffn_kernel,
        grid=(r // tm, f // tf),
        in_specs=[
            pl.BlockSpec((tm, d), lambda i, j: (i, 0)),
            pl.BlockSpec((d, tf), lambda i, j: (0, j)),
            pl.BlockSpec((d, tf), lambda i, j: (0, j)),
            pl.BlockSpec((tf, d), lambda i, j: (j, 0)),
            pl.BlockSpec((tm, d), lambda i, j: (i, 0)),
            pl.BlockSpec((1, 6, d), mod_map),
            pl.BlockSpec((1, d), lambda i, j: (0, 0)),
        ],
        out_specs=pl.BlockSpec((tm, d), lambda i, j: (i, 0)),
        out_shape=jax.ShapeDtypeStruct((r, d), F32),
        scratch_shapes=[pltpu.VMEM((tm, d), F32)],
        compiler_params=_cparams(("parallel", "arbitrary")),
        name="ffn_dense",
    )(h2.reshape(r, d), wg, wu, wd, xs.reshape(r, d), mod_l, post_gain.reshape(1, d))
    return out.reshape(b, n, d)


def _moe_kernel(ie_ref, it_ref, ilo_ref, ihi_ref, ifirst_ref, ivalid_ref, iflush_ref,
                idx0_ref, idx_next_ref, dst_prev_ref, h_hbm, wg_ref, wu_ref, wd_ref, y_hbm,
                xg, ys, xb, acc, gsem, ssem, *, tm, sub, rps, nf):
    w = pl.program_id(0)
    j = pl.program_id(1)
    tile = it_ref[w]
    slot = tile % 2
    oslot = 1 - slot

    def gather_copy(ids_ref, dslot, r):
        src = h_hbm.at[pl.ds(pl.multiple_of(ids_ref[0, 0, r] * sub, sub), sub), :]
        dst = xg.at[dslot, pl.ds(pl.multiple_of(r * sub, sub), sub), :]
        return pltpu.make_async_copy(src, dst, gsem.at[dslot])

    def scatter_copy(r):
        src = ys.at[oslot, pl.ds(pl.multiple_of(r * sub, sub), sub), :]
        dst = y_hbm.at[pl.ds(pl.multiple_of(dst_prev_ref[0, 0, r] * sub, sub), sub), :]
        return pltpu.make_async_copy(src, dst, ssem.at[oslot])

    def wait_gather(dslot):
        pltpu.make_async_copy(h_hbm.at[pl.ds(0, tm * sub), :], xg.at[dslot], gsem.at[dslot]).wait()

    def wait_scatter():
        pltpu.make_async_copy(ys.at[oslot], y_hbm.at[pl.ds(0, tm * sub), :], ssem.at[oslot]).wait()

    @pl.when(j == 0)
    def _():
        @pl.when(w == 0)
        def _():
            def issue(r, carry):
                gather_copy(idx0_ref, 0, r).start()
                return carry
            lax.fori_loop(0, tm, issue, 0)
            ys[...] = jnp.zeros_like(ys)
            wait_gather(0)

        @pl.when(jnp.logical_and(w > 0, ivalid_ref[jnp.maximum(w - 1, 0)] > 0))
        def _():
            wait_gather((it_ref[jnp.maximum(w - 1, 0)] + 1) % 2)

        @pl.when(ifirst_ref[w] > 0)
        def _():
            for s in range(sub):
                xb[:, s * LANES:(s + 1) * LANES] = xg[slot, pl.ds(s, tm, stride=sub), :].astype(BF16)

        acc[...] = jnp.zeros_like(acc)

        @pl.when(iflush_ref[w] > 0)
        def _():
            def issue(r, carry):
                scatter_copy(r).start()
                return carry
            lax.fori_loop(0, tm, issue, 0)
            wait_scatter()

    @pl.when(ivalid_ref[w] > 0)
    def _():
        base = j * rps
        for k in range(rps):
            gather_copy(idx_next_ref, oslot, base + k).start()
            scatter_copy(base + k).start()
        h = xb[...]
        g = jnp.dot(h, wg_ref[0].astype(BF16), preferred_element_type=F32)
        u = jnp.dot(h, wu_ref[0].astype(BF16), preferred_element_type=F32)
        acc[...] += jnp.dot((_silu(g) * u).astype(BF16), wd_ref[0].astype(BF16),
                            preferred_element_type=F32)

        @pl.when(j == nf - 1)
        def _():
            for r in range(rps * nf, tm):
                gather_copy(idx_next_ref, oslot, r).start()
                scatter_copy(r).start()
            wait_scatter()
            whole = jnp.logical_and(ilo_ref[w] == 0, ihi_ref[w] == tm)

            @pl.when(whole)
            def _():
                for s in range(sub):
                    ys[slot, pl.ds(s, tm, stride=sub), :] = acc[:, s * LANES:(s + 1) * LANES]

            @pl.when(jnp.logical_not(whole))
            def _():
                rowi = lax.broadcasted_iota(jnp.int32, (tm, LANES), 0)
                mine = jnp.logical_and(rowi >= ilo_ref[w], rowi < ihi_ref[w])
                for s in range(sub):
                    old = ys[slot, pl.ds(s, tm, stride=sub), :]
                    ys[slot, pl.ds(s, tm, stride=sub), :] = jnp.where(
                        mine, acc[:, s * LANES:(s + 1) * LANES], old)


def _combine_kernel(y_ref, route_ref, x_ref, mod_ref, post_ref, o_ref, *, tm, sub):
    rt = route_ref[...]
    stride = TOP_K * sub
    y1 = jnp.concatenate([y_ref[pl.ds(s, tm, stride=stride), :] for s in range(sub)], axis=-1)
    y2 = jnp.concatenate([y_ref[pl.ds(sub + s, tm, stride=stride), :] for s in range(sub)], axis=-1)
    y = rt[:, 2:3] * y1 + rt[:, 3:4] * y2
    m = mod_ref[0]
    o_ref[...] = x_ref[...] + m[5:6] * _rms(y, post_ref[...])


def _ffn_moe(hrows, route, xs, mod_l, mod_row, post_gain, wg, wu, wd):
    b, n, d = xs.shape
    r = b * n
    n_pairs = TOP_K * r
    n_exp, _, f = wg.shape
    sub = d // LANES
    tm = _pick_tile(n_pairs, 1024, SUBLANES)
    tf = _pick_tile(f, 512, LANES)
    nf = f // tf
    rps = tm // nf
    n_tiles = n_pairs // tm
    n_items = n_tiles + n_exp

    e_flat = route[:, :TOP_K].astype(jnp.int32).reshape(-1)
    pair_ids = jnp.arange(n_pairs, dtype=jnp.int32)
    order = jnp.sort(e_flat * n_pairs + pair_ids) % n_pairs
    counts = jnp.sum((e_flat[:, None] == jnp.arange(n_exp, dtype=jnp.int32)[None, :]).astype(jnp.int32),
                     axis=0)
    cend = jnp.cumsum(counts)
    cstart = cend - counts
    first_tile = cstart // tm
    last_tile = jnp.maximum(cend - 1, 0) // tm
    n_e = jnp.where(counts > 0, last_tile - first_tile + 1, 0)
    item_end = jnp.cumsum(n_e)
    item_start = item_end - n_e
    total = item_end[-1]
    wi = jnp.arange(n_items, dtype=jnp.int32)
    item_e = jnp.minimum(jnp.sum((wi[:, None] >= item_end[None, :]).astype(jnp.int32), axis=1), n_exp - 1)
    item_valid = (wi < total).astype(jnp.int32)
    item_tile = jnp.where(item_valid > 0, first_tile[item_e] + wi - item_start[item_e], n_tiles)
    item_lo = jnp.clip(cstart[item_e] - item_tile * tm, 0, tm) * item_valid
    item_hi = jnp.clip(cend[item_e] - item_tile * tm, 0, tm) * item_valid
    prev_tile = jnp.concatenate([jnp.full((1,), -1, jnp.int32), item_tile[:-1]])
    item_first = jnp.logical_and(item_valid > 0, item_tile != prev_tile).astype(jnp.int32)
    item_flush = (wi == total).astype(jnp.int32)
    last_e = item_e[jnp.maximum(total - 1, 0)]
    item_e = jnp.where(item_valid > 0, item_e, last_e)

    src3 = (order // TOP_K).reshape(n_tiles, 1, tm)
    dst3 = order.reshape(n_tiles, 1, tm)

    def w_col(w, j, ie, it, ilo, ihi, ifi, iva, ifl):
        return (ie[w], 0, jnp.where(iva[w] > 0, j, nf - 1))

    def w_row(w, j, ie, it, ilo, ihi, ifi, iva, ifl):
        return (ie[w], jnp.where(iva[w] > 0, j, nf - 1), 0)

    smem = pltpu.MemorySpace.SMEM
    y2 = pl.pallas_call(
        functools.partial(_moe_kernel, tm=tm, sub=sub, rps=rps, nf=nf),
        grid_spec=pltpu.PrefetchScalarGridSpec(
            num_scalar_prefetch=7,
            grid=(n_items, nf),
            in_specs=[
                pl.BlockSpec((1, 1, tm), lambda w, j, *_: (0, 0, 0), memory_space=smem),
                pl.BlockSpec((1, 1, tm), lambda w, j, ie, it, *_: (jnp.minimum(it[w] + 1, n_tiles - 1), 0, 0),
                             memory_space=smem),
                pl.BlockSpec((1, 1, tm), lambda w, j, ie, it, *_: (jnp.clip(it[w] - 1, 0, n_tiles - 1), 0, 0),
                             memory_space=smem),
                pl.BlockSpec(memory_space=pl.ANY),
                pl.BlockSpec((1, d, tf), w_col),
                pl.BlockSpec((1, d, tf), w_col),
                pl.BlockSpec((1, tf, d), w_row),
            ],
            out_specs=pl.BlockSpec(memory_space=pl.ANY),
            scratch_shapes=[
                pltpu.VMEM((2, tm * sub, LANES), F32),
                pltpu.VMEM((2, tm * sub, LANES), F32),
                pltpu.VMEM((tm, d), BF16),
                pltpu.VMEM((tm, d), F32),
                pltpu.SemaphoreType.DMA((2,)),
                pltpu.SemaphoreType.DMA((2,)),
            ],
        ),
        out_shape=jax.ShapeDtypeStruct((n_pairs * sub, LANES), F32),
        compiler_params=_cparams(("arbitrary", "arbitrary")),
        name="moe_experts",
    )(item_e, item_tile, item_lo, item_hi, item_first, item_valid, item_flush,
      src3, src3, dst3, hrows, wg, wu, wd)

    tmc = _pick_tile(n, 512, SUBLANES)
    tiles_per_batch = n // tmc
    if mod_row is None:
        mod_map = lambda i: (i // tiles_per_batch, 0, 0)
    else:
        mod_map = lambda i: (mod_row, 0, 0)
    out = pl.pallas_call(
        functools.partial(_combine_kernel, tm=tmc, sub=sub),
        grid=(r // tmc,),
        in_specs=[
            pl.BlockSpec((tmc * TOP_K * sub, LANES), lambda i: (i, 0)),
            pl.BlockSpec((tmc, LANES), lambda i: (i, 0)),
            pl.BlockSpec((tmc, d), lambda i: (i, 0)),
            pl.BlockSpec((1, 6, d), mod_map),
            pl.BlockSpec((1, d), lambda i: (0, 0)),
        ],
        out_specs=pl.BlockSpec((tmc, d), lambda i: (i, 0)),
        out_shape=jax.ShapeDtypeStruct((r, d), F32),
        compiler_params=_cparams(("parallel",)),
        name="moe_combine",
    )(y2, route, xs.reshape(r, d), mod_l, post_gain.reshape(1, d))
    return out.reshape(b, n, d)


def kernel(x, c, ctx, c_ctx, ada_w, ada_b, mix_pre, mix_post, ffn_pre, ffn_post, w_in, hgrn_lb, hgrn_gain,
           pool_w, pool_scale, w_branch_rec, w_branch_pool, w_out, ffn_w_gate, ffn_w_up, ffn_w_down,
           moe_router, moe_w_gate, moe_w_up, moe_w_down):
    b, n, d = x.shape
    depth = ada_w.shape[0]
    assert b + 1 <= MOD_ROWS and d % LANES == 0
    ctx_row = b

    cond = jnp.zeros((MOD_ROWS, d), F32).at[:b].set(c).at[b].set(c_ctx)
    mod = _modulation(cond, ada_w, ada_b).reshape(depth, MOD_ROWS, 6, d)

    lb_all = jnp.cumsum(jax.nn.softmax(hgrn_lb.astype(F32), axis=0), axis=0)
    lb_all = lb_all - lb_all[0:1]

    bf = lambda a: a.astype(BF16)
    w_in_b, pool_w_b = bf(w_in), bf(pool_w)
    w_rec_b, w_pool_b, w_out_b = bf(w_branch_rec), bf(w_branch_pool), bf(w_out)
    ffn_g, ffn_u, ffn_d = bf(ffn_w_gate), bf(ffn_w_up), bf(ffn_w_down)
    moe_g, moe_u, moe_d = moe_w_gate, moe_w_up, moe_w_down

    u_blk = (5 * D_HGRN) // D_POOL
    s_zero = jnp.zeros((b, HEADS, HEAD_DIM, HEAD_DIM), F32)

    def mixer(xs, l, mod_row, s0f, s0b, on_grid, router):
        p = _in_proj(xs, mod[l], mod_row, mix_pre[l], w_in_b[l])
        o, s_f, s_b = _gla(p, lb_all[l], s0f, s0b, with_out=True, col0=0)
        yp = _pool(p, pool_w_b[l], pool_scale[l], on_grid=on_grid, col_block=u_blk)
        outs = _merge(o, p, yp, xs, mod[l], mod_row, hgrn_gain[l], mix_post[l], ffn_pre[l],
                      w_rec_b[l], w_pool_b[l], w_out_b[l], router=router)
        return outs, s_f, s_b

    def channel(outs, l, mod_row):
        j = l // 2
        if l % 2 == 0:
            xn, h2 = outs
            return _ffn_dense(h2, xn, mod[l], mod_row, ffn_post[l], ffn_g[j], ffn_u[j], ffn_d[j])
        xn, hrows, route = outs
        return _ffn_moe(hrows, route, xn, mod[l], mod_row, ffn_post[l], moe_g[j], moe_u[j], moe_d[j])

    cx = ctx
    for l in range(depth):
        last = l == depth - 1
        router = moe_router[l // 2] if l % 2 == 1 else None
        if last:
            pc = _in_proj(cx, mod[l], ctx_row, mix_pre[l], w_in_b[l][:, D_HGRN:4 * D_HGRN])
            s_f, s_b = _gla(pc, lb_all[l], s_zero, s_zero, with_out=False, col0=0)
            c_outs = None
        else:
            c_outs, s_f, s_b = mixer(cx, l, ctx_row, s_zero, s_zero, False, router)
        x_outs, _, _ = mixer(x, l, None, s_f, s_b, True, router)
        x = channel(x_outs, l, None)
        if not last:
            cx = channel(c_outs, l, ctx_row)
    return x
```

```python
import functools

import jax
import jax.numpy as jnp
from jax import lax
from jax.experimental import pallas as pl
from jax.experimental.pallas import tpu as pltpu

F32 = jnp.float32
BF16 = jnp.bfloat16

EPS = 1e-6
LANES = 128
SUBLANES = 8
HEADS = 4
HEAD_DIM = 128
D_HGRN = HEADS * HEAD_DIM
D_POOL = 512
POOL_WINDOWS = (2, 4, 8, 16)
POOL_GROUP = D_POOL // len(POOL_WINDOWS)
GRID_W = 64
CHUNK = 64
SUBCHUNK = CHUNK // 2
SAFE_DECAY = 85.0
DIAG = SUBLANES
GLA_HEADS_PER_STEP = 2
STATE_UNROLL = 8
ATT_UNROLL = 4
TOP_K = 2
MOD_ROWS = 40
VMEM_LIMIT = 56 * 1024 * 1024

def _cparams(sem):
    return pltpu.CompilerParams(dimension_semantics=sem, vmem_limit_bytes=VMEM_LIMIT)


def _pick_tile(n, cap, mult):
    if n <= cap:
        return n
    best = None
    for t in range(mult, cap + 1, mult):
        if n % t == 0:
            best = t
    assert best is not None, (n, cap, mult)
    return best


def _sigmoid(z):
    return 1.0 / (1.0 + jnp.exp(-z))


def _silu(z):
    return z * _sigmoid(z)


def _mod_kernel(c_ref, w_ref, b_ref, o_ref):
    a = _silu(c_ref[...])
    o_ref[0] = jnp.dot(a, w_ref[0], preferred_element_type=F32,
                       precision=lax.Precision.HIGHEST) + b_ref[0]


def _modulation(cond, ada_w, ada_b):
    depth, d, six_d = ada_w.shape
    tn = _pick_tile(six_d, 1024, LANES)
    return pl.pallas_call(
        _mod_kernel,
        grid=(depth, six_d // tn),
        in_specs=[
            pl.BlockSpec((MOD_ROWS, d), lambda l, j: (0, 0)),
            pl.BlockSpec((1, d, tn), lambda l, j: (l, 0, j)),
            pl.BlockSpec((1, 1, tn), lambda l, j: (l, 0, j)),
        ],
        out_specs=pl.BlockSpec((1, MOD_ROWS, tn), lambda l, j: (l, 0, j)),
        out_shape=jax.ShapeDtypeStruct((depth, MOD_ROWS, six_d), F32),
        compiler_params=_cparams(("parallel", "parallel")),
        name="modulation",
    )(cond, ada_w, ada_b.reshape(depth, 1, six_d))


def _rms(x, gain):
    ms = jnp.mean(x * x, axis=-1, keepdims=True)
    return x * lax.rsqrt(ms + EPS) * gain


def _inproj_kernel(x_ref, mod_ref, gain_ref, w_ref, o_ref, *, tn):
    m = mod_ref[0]
    h = _rms(x_ref[0], gain_ref[...]) * (1.0 + m[1:2]) + m[0:1]
    hb = h.astype(BF16)
    for j in range(w_ref.shape[1] // tn):
        o_ref[0, :, j * tn:(j + 1) * tn] = jnp.dot(
            hb, w_ref[:, j * tn:(j + 1) * tn], preferred_element_type=F32).astype(o_ref.dtype)


def _in_proj(xs, mod_l, mod_row, gain, w):
    b, n, d = xs.shape
    nc = w.shape[1]
    tm = _pick_tile(n, 512, SUBLANES)
    tn = _pick_tile(nc, 512, LANES)
    if mod_row is None:
        mod_map = lambda bi, i: (bi, 0, 0)
    else:
        mod_map = lambda bi, i: (mod_row, 0, 0)
    return pl.pallas_call(
        functools.partial(_inproj_kernel, tn=tn),
        grid=(b, n // tm),
        in_specs=[
            pl.BlockSpec((1, tm, d), lambda bi, i: (bi, i, 0)),
            pl.BlockSpec((1, 6, d), mod_map),
            pl.BlockSpec((1, d), lambda bi, i: (0, 0)),
            pl.BlockSpec((d, nc), lambda bi, i: (0, 0)),
        ],
        out_specs=pl.BlockSpec((1, tm, nc), lambda bi, i: (bi, i, 0)),
        out_shape=jax.ShapeDtypeStruct((b, n, nc), BF16),
        compiler_params=_cparams(("parallel", "parallel")),
        name="in_proj",
    )(xs, mod_l, gain.reshape(1, d), w)


def _neg_abs(a):
    bits = lax.bitcast_convert_type(a, jnp.uint32) | jnp.uint32(0x80000000)
    return lax.bitcast_convert_type(bits, F32)


def _dot_nt(a, b):
    return lax.dot_general(a, b, (((1,), (1,)), ((), ())), preferred_element_type=F32)


def _dot_tn(a, b):
    return lax.dot_general(a, b, (((0,), (0,)), ((), ())), preferred_element_type=F32)


def _gla_kernel(*refs, n, with_out, hpb):
    if with_out:
        (q_ref, ff_ref, fb_ref, i_ref, lb_ref, s0f_ref, s0b_ref,
         o_ref, sf_ref, sb_ref, qs, qf_f, qf_b, vs, cum_f, cum_b, k_f, k_b, kh_f, kh_b) = refs
    else:
        (ff_ref, fb_ref, i_ref, lb_ref, s0f_ref, s0b_ref,
         sf_ref, sb_ref, vs, cum_f, cum_b, k_f, k_b, kh_f, kh_b) = refs
    c = CHUNK
    nchunks = n // c
    nblk = c // DIAG

    hb = SUBCHUNK
    pos_c = lax.broadcasted_iota(jnp.int32, (c, hpb * HEAD_DIM), 0) % hb

    def prepare(ci, low):
        rows = pl.ds(pl.multiple_of(ci * c, c), c)
        if with_out:
            qs[rows, :] = _silu(q_ref[0, rows, :].astype(F32)) * (HEAD_DIM ** -0.5)
        vs[rows, :] = i_ref[0, rows, :].astype(F32)
        for rev in (False, True):
            z = (fb_ref if rev else ff_ref)[0, rows, :].astype(F32)
            lb = lb_ref[1:2, :] if rev else lb_ref[0:1, :]
            e = jnp.exp(-jnp.abs(z))
            log_sig = jnp.minimum(z, 0.0) - jnp.log(1.0 + e)
            a = jnp.log(lb)
            bt = jnp.log(1.0 - lb) + log_sig
            mx = jnp.maximum(a, bt)
            mn = jnp.minimum(a, bt)
            s = mx + jnp.log(1.0 + jnp.exp(mn - mx))
            (k_b if rev else k_f)[rows, :] = (1.0 - lb) * (jnp.where(z >= 0.0, e, 1.0) / (1.0 + e))
            sh = 1
            while sh < hb:
                if rev:
                    s = s + jnp.where(pos_c < hb - sh, pltpu.roll(s, c - sh, 0), 0.0)
                else:
                    s = s + jnp.where(pos_c >= sh, pltpu.roll(s, sh, 0), 0.0)
                sh *= 2
            (cum_b if rev else cum_f)[rows, :] = s
            low = jnp.minimum(low, s)
        return low

    lowest = jnp.min(lax.fori_loop(0, nchunks, prepare, jnp.zeros((c, hpb * HEAD_DIM), F32)))

    row = lax.broadcasted_iota(jnp.int32, (c, c), 0)
    col = lax.broadcasted_iota(jnp.int32, (c, c), 1)
    rowv = lax.broadcasted_iota(jnp.int32, (c, HEAD_DIM), 0)
    row3 = lax.broadcasted_iota(jnp.int32, (nblk, DIAG, HEAD_DIM), 1)

    def level_map(rev):
        lv = jnp.zeros((c, c), jnp.int32)
        m = c
        while m > 1:
            half = m // 2
            t_right = (row % m) >= half
            s_right = (col % m) >= half
            pair = jnp.logical_and(jnp.logical_not(t_right), s_right) if rev else \
                jnp.logical_and(t_right, jnp.logical_not(s_right))
            lv = jnp.where(jnp.logical_and((row // m) == (col // m), pair), m, lv)
            m = half
        return lv

    lv_f, lv_b = level_map(False), level_map(True)
    t_rows_of = {}
    m = c
    while m > 1:
        in_right = (rowv % m) >= m // 2
        t_rows_of[(False, m)] = in_right
        t_rows_of[(True, m)] = jnp.logical_not(in_right)
        m //= 2

    far_of = {False: rowv >= hb, True: rowv < hb}

    def one_chunk(rev, cidx, st, accumulate, hh):
        r0 = pl.multiple_of(cidx * c, c)
        hs = slice(hh * HEAD_DIM, (hh + 1) * HEAD_DIM)
        loc = (cum_b if rev else cum_f)[pl.ds(r0, c), hs]
        kc = (k_b if rev else k_f)[pl.ds(r0, c), hs]
        vc = vs[pl.ds(r0, c), hs]
        vcb = vc.astype(BF16)
        far = far_of[rev]
        first = loc[hb:hb + 1, :] if rev else loc[hb - 1:hb, :]
        end = loc[0:1, :] if rev else loc[c - 1:c, :]
        tot = first + end

        def put(o):
            if accumulate:
                o_ref[0, pl.ds(r0, c), hs] = o_ref[0, pl.ds(r0, c), hs] + o
            else:
                o_ref[0, pl.ds(r0, c), hs] = o

        cum = loc + jnp.where(far, first, 0.0)
        if with_out:
            qc = qs[pl.ds(r0, c), hs]
            o = _dot_nt((qc * jnp.exp(cum)).astype(BF16), st.astype(BF16))
            lv = lv_b if rev else lv_f
            c3 = cum.reshape(nblk, DIAG, HEAD_DIM)
            att = jnp.zeros((c, c), F32)
            m = c
            while m > 1:
                half = m // 2
                t_rows = t_rows_of[(rev, m)]
                if m > DIAG:
                    pieces = []
                    for blk in range(c // m):
                        rr = blk * m + (half if rev else half - 1)
                        pieces.append(jnp.broadcast_to(cum[rr:rr + 1, :], (m, HEAD_DIM)))
                    ref = pieces[0] if len(pieces) == 1 else jnp.concatenate(pieces, axis=0)
                elif m > 2:
                    ref3 = None
                    for grp in reversed(range(DIAG // m)):
                        rr = grp * m + (half if rev else half - 1)
                        piece = jnp.broadcast_to(c3[:, rr:rr + 1, :], (nblk, DIAG, HEAD_DIM))
                        ref3 = piece if ref3 is None else jnp.where(row3 < (grp + 1) * m, piece, ref3)
                    ref = ref3.reshape(c, HEAD_DIM)
                else:
                    ref = jnp.where(t_rows, pltpu.roll(cum, (c - 1) if rev else 1, 0), cum)
                w = jnp.where(t_rows, qc, kc) * jnp.exp(_neg_abs(cum - ref))
                wb = w.astype(BF16)
                att = jnp.where(lv == m, _dot_nt(wb, wb), att)
                m = half
            o = o + jnp.dot(att.astype(BF16), vcb, preferred_element_type=F32)
            put(o + jnp.sum(qc * kc, axis=-1, keepdims=True) * vc)

        khat = kc * jnp.exp(tot - cum)
        return jnp.exp(tot) * st + _dot_tn(vcb, khat.astype(BF16))

    def make_body(accumulate):
        def body(ci, carry):
            st_f = tuple(one_chunk(False, ci, carry[hh], accumulate, hh) for hh in range(hpb))
            st_b = tuple(one_chunk(True, nchunks - 1 - ci, carry[hpb + hh], accumulate, hh)
                         for hh in range(hpb))
            return st_f + st_b
        return body

    def initial_states():
        return tuple(s0f_ref[0, hh] for hh in range(hpb)) + tuple(s0b_ref[0, hh] for hh in range(hpb))

    def put_states(carry):
        for hh in range(hpb):
            sf_ref[0, hh] = carry[hh]
            sb_ref[0, hh] = carry[hpb + hh]

    def sweep_midpoint():
        carry = lax.fori_loop(0, nchunks // 2, make_body(False), initial_states())
        put_states(lax.fori_loop(nchunks // 2, nchunks, make_body(True), carry))

    ab = 2 * c
    rowa = lax.broadcasted_iota(jnp.int32, (ab, ab), 0)
    cola = lax.broadcasted_iota(jnp.int32, (ab, ab), 1)
    rowb = lax.broadcasted_iota(jnp.int32, (ab, HEAD_DIM), 0)

    same_chunk = (rowa // c) == (cola // c)
    keep_of = {False: jnp.logical_and(same_chunk, cola <= rowa),
               True: jnp.logical_and(same_chunk, cola >= rowa)}
    far_b = {False: (rowb % c) >= hb, True: (rowb % c) < hb}

    def chunk_rows(x, r):
        return jnp.concatenate(
            [jnp.broadcast_to(x[ch * c + r:ch * c + r + 1, :], (c, HEAD_DIM)) for ch in range(ab // c)],
            axis=0)

    def att_block(rev, r0, hh):
        hs = slice(hh * HEAD_DIM, (hh + 1) * HEAD_DIM)
        loc = (cum_b if rev else cum_f)[pl.ds(r0, ab), hs]
        far = far_b[rev]
        e_first = jnp.exp(chunk_rows(loc, hb if rev else hb - 1))
        e_end = jnp.exp(chunk_rows(loc, 0 if rev else c - 1))
        k_loc = (k_b if rev else k_f)[pl.ds(r0, ab), hs] * jnp.exp(-loc)
        (kh_b if rev else kh_f)[pl.ds(r0, ab), hs] = (
            k_loc * jnp.where(far, e_end, e_end * e_first)).astype(BF16)
        if not with_out:
            return None
        q_loc = qs[pl.ds(r0, ab), hs] * jnp.exp(loc)
        (qf_b if rev else qf_f)[pl.ds(r0, ab), hs] = (q_loc * jnp.where(far, e_first, 1.0)).astype(BF16)
        q2 = jnp.concatenate([jnp.where(far, 0.0, q_loc), jnp.where(far, q_loc, 0.0)], axis=1)
        k2 = jnp.concatenate([k_loc, jnp.where(far, k_loc, k_loc * e_first)], axis=1)
        att = jnp.where(keep_of[rev], _dot_nt(q2.astype(BF16), k2.astype(BF16)), 0.0)
        return jnp.dot(att.astype(BF16), vs[pl.ds(r0, ab), hs].astype(BF16), preferred_element_type=F32)

    att_unroll = min(ATT_UNROLL, n // ab)
    state_unroll = min(STATE_UNROLL, nchunks)

    def att_body(gi, carry):
        for u in range(att_unroll):
            r0 = pl.multiple_of((gi * att_unroll + u) * ab, ab)
            for hh in range(hpb):
                o_f = att_block(False, r0, hh)
                o_b = att_block(True, r0, hh)
                if with_out:
                    o_ref[0, pl.ds(r0, ab), hh * HEAD_DIM:(hh + 1) * HEAD_DIM] = o_f + o_b
        return carry

    def state_chunk(rev, cidx, st, hh):
        r0 = pl.multiple_of(cidx * c, c)
        hs = slice(hh * HEAD_DIM, (hh + 1) * HEAD_DIM)
        sums_ref = cum_b if rev else cum_f

        def row_at(off):
            grp = off // SUBLANES * SUBLANES
            blk = sums_ref[pl.ds(pl.multiple_of(r0 + grp, SUBLANES), SUBLANES), hs]
            return blk[off - grp:off - grp + 1, :]

        first = row_at(hb if rev else hb - 1)
        end = row_at(0 if rev else c - 1)
        if with_out:
            qf = (qf_b if rev else qf_f)[pl.ds(r0, c), hs]
            (k_b if rev else k_f)[pl.ds(r0, c), hs] = _dot_nt(qf, st.astype(BF16))
        kh = (kh_b if rev else kh_f)[pl.ds(r0, c), hs]
        return jnp.exp(first + end) * st + _dot_tn(vs[pl.ds(r0, c), hs].astype(BF16), kh)

    def state_body(gi, carry):
        for u in range(state_unroll):
            ci = gi * state_unroll + u
            st_f = tuple(state_chunk(False, ci, carry[hh], hh) for hh in range(hpb))
            st_b = tuple(state_chunk(True, nchunks - 1 - ci, carry[hpb + hh], hh) for hh in range(hpb))
            carry = st_f + st_b
        return carry

    def add_state_part(ci, carry):
        rows = pl.ds(pl.multiple_of(ci * c, c), c)
        o_ref[0, rows, :] = o_ref[0, rows, :] + k_f[rows, :] + k_b[rows, :]
        return carry

    def sweep_one_sided():
        lax.fori_loop(0, n // (ab * att_unroll), att_body, 0)
        put_states(lax.fori_loop(0, nchunks // state_unroll, state_body, initial_states()))
        if with_out:
            lax.fori_loop(0, nchunks, add_state_part, 0)

    in_range = lowest >= -SAFE_DECAY

    @pl.when(in_range)
    def _():
        sweep_one_sided()

    @pl.when(jnp.logical_not(in_range))
    def _():
        sweep_midpoint()


def _gla(p, lb, s0f, s0b, *, with_out, col0):
    b, n, _ = p.shape
    assert n % (2 * CHUNK) == 0
    assert (n // CHUNK) % min(STATE_UNROLL, n // CHUNK) == 0
    assert (n // (2 * CHUNK)) % min(ATT_UNROLL, n // (2 * CHUNK)) == 0
    hpb = GLA_HEADS_PER_STEP
    groups = HEADS // hpb
    wblk = hpb * HEAD_DIM
    assert col0 % hpb == 0

    def col_spec(k):
        return pl.BlockSpec((1, n, wblk), lambda bi, h, k=k: (bi, 0, col0 // hpb + k * groups + h))

    st_spec = pl.BlockSpec((1, hpb, HEAD_DIM, HEAD_DIM), lambda bi, h: (bi, h, 0, 0))
    n_cols = 4 if with_out else 3
    in_specs = [col_spec(k) for k in range(n_cols)] + [
        pl.BlockSpec((2, wblk), lambda bi, h: (0, h)), st_spec, st_spec]
    st_shape = jax.ShapeDtypeStruct((b, HEADS, HEAD_DIM, HEAD_DIM), F32)
    out_specs = [st_spec, st_spec]
    out_shape = [st_shape, st_shape]
    scratch = [pltpu.VMEM((n, wblk), F32) for _ in range(5)] + [pltpu.VMEM((n, wblk), BF16)] * 2
    if with_out:
        out_specs = [pl.BlockSpec((1, n, wblk), lambda bi, h: (bi, 0, h))] + out_specs
        out_shape = [jax.ShapeDtypeStruct((b, n, D_HGRN), F32)] + out_shape
        scratch = [pltpu.VMEM((n, wblk), F32)] + [pltpu.VMEM((n, wblk), BF16)] * 2 + scratch
    return pl.pallas_call(
        functools.partial(_gla_kernel, n=n, with_out=with_out, hpb=hpb),
        grid=(b, groups),
        in_specs=in_specs,
        out_specs=out_specs,
        out_shape=out_shape,
        scratch_shapes=scratch,
        compiler_params=_cparams(("parallel", "parallel")),
        name="gla" if with_out else "gla_state",
    )(*([p] * n_cols), lb, s0f, s0b)


def _pool_kernel(u_ref, pw_ref, ps_ref, o_ref, *, n, width):
    rows = n // width
    t = lax.broadcasted_iota(jnp.int32, (n, POOL_GROUP), 0)
    colp = t % width
    rowp = t // width

    def shift(a, delta, pos, length, stride):
        rolled = pltpu.roll(a, (delta * stride) % n, 0)
        ok = jnp.logical_and(pos - delta >= 0, pos - delta < length)
        return jnp.where(ok, rolled, 0.0)

    def box_sum(a, w, pos, length, stride):
        trail, lead = a, a
        h = 1
        while h < w // 2:
            trail = trail + shift(trail, h, pos, length, stride)
            lead = lead + shift(lead, -h, pos, length, stride)
            h *= 2
        return shift(trail, 1, pos, length, stride) + lead

    def count(pos, w, length):
        lo = jnp.clip(pos - w // 2, 0, length)
        hi = jnp.clip(pos - w // 2 + w, 0, length)
        return (hi - lo).astype(F32)

    for gi, w in enumerate(POOL_WINDOWS):
        ug = u_ref[0, :, gi * POOL_GROUP:(gi + 1) * POOL_GROUP].astype(F32)
        if rows > 1:
            s = box_sum(ug, w, rowp, rows, width) / count(rowp, w, rows)
            s = box_sum(s, w, colp, width, 1) / count(colp, w, width)
        else:
            s = box_sum(ug, w, colp, width, 1) / count(colp, w, width)
        y = jnp.dot((s - ug).astype(BF16), pw_ref[gi], preferred_element_type=F32)
        y = y * ps_ref[:, gi * POOL_GROUP:(gi + 1) * POOL_GROUP]
        o_ref[0, :, gi * POOL_GROUP:(gi + 1) * POOL_GROUP] = y.astype(o_ref.dtype)


def _pool(p, pool_w, pool_scale, *, on_grid, col_block):
    b, n, _ = p.shape
    width = GRID_W if on_grid else n
    assert n % width == 0 and width & (width - 1) == 0
    return pl.pallas_call(
        functools.partial(_pool_kernel, n=n, width=width),
        grid=(b,),
        in_specs=[
            pl.BlockSpec((1, n, D_POOL), lambda bi: (bi, 0, col_block)),
            pl.BlockSpec(pool_w.shape, lambda bi: (0, 0, 0)),
            pl.BlockSpec((1, D_POOL), lambda bi: (0, 0)),
        ],
        out_specs=pl.BlockSpec((1, n, D_POOL), lambda bi: (bi, 0, 0)),
        out_shape=jax.ShapeDtypeStruct((b, n, D_POOL), BF16),
        compiler_params=_cparams(("parallel",)),
        name="pool",
    )(p, pool_w, pool_scale.reshape(1, D_POOL))


def _merge_kernel(*refs, moe, n_experts):
    (o_ref, g_ref, yp_ref, ga_ref, gb_ref, x_ref, mod_ref, og_ref, post_ref, pre2_ref,
     wr_ref, wp_ref, wo_ref) = refs[:13]
    if moe:
        router_ref, xn_ref, hrow_ref, route_ref = refs[13:]
    else:
        xn_ref, h2_ref = refs[13:]
    tm = x_ref.shape[1]
    m = mod_ref[0]
    o = o_ref[0]
    parts = []
    for h in range(HEADS):
        oh = o[:, h * HEAD_DIM:(h + 1) * HEAD_DIM]
        parts.append(oh * lax.rsqrt(jnp.mean(oh * oh, axis=-1, keepdims=True) + EPS))
    on = jnp.concatenate(parts, axis=-1) * og_ref[...]
    y_rec = (on * _silu(g_ref[0].astype(F32))).astype(BF16)
    rec = jnp.dot(y_rec, wr_ref[...], preferred_element_type=F32)
    pool = jnp.dot(yp_ref[0], wp_ref[...], preferred_element_type=F32)
    merged = _sigmoid(ga_ref[0].astype(F32)) * rec + _sigmoid(gb_ref[0].astype(F32)) * pool
    y = jnp.dot(merged.astype(BF16), wo_ref[...], preferred_element_type=F32)
    xn = x_ref[0] + m[2:3] * _rms(y, post_ref[...])
    xn_ref[0] = xn
    h2 = _rms(xn, pre2_ref[...]) * (1.0 + m[4:5]) + m[3:4]
    if not moe:
        h2_ref[0] = h2.astype(h2_ref.dtype)
        return
    d = h2.shape[-1]
    for s in range(d // LANES):
        hrow_ref[pl.ds(s, tm, stride=d // LANES), :] = h2[:, s * LANES:(s + 1) * LANES]
    h_hi = h2.astype(BF16)
    h_lo = (h2 - h_hi.astype(F32)).astype(BF16)
    l2 = jnp.dot(h_hi, router_ref[...], preferred_element_type=F32)
    logits = (l2[:, :LANES] + l2[:, LANES:]
              + jnp.dot(h_lo, router_ref[:, :LANES], preferred_element_type=F32))
    lane = lax.broadcasted_iota(jnp.int32, logits.shape, 1)
    neg = jnp.float32(-jnp.inf)
    lg = jnp.where(lane < n_experts, logits, neg)
    m1 = jnp.max(lg, axis=-1, keepdims=True)
    i1 = jnp.min(jnp.where(lg == m1, lane, LANES), axis=-1, keepdims=True)
    lg2 = jnp.where(lane == i1, neg, lg)
    m2 = jnp.max(lg2, axis=-1, keepdims=True)
    i2 = jnp.min(jnp.where(lg2 == m2, lane, LANES), axis=-1, keepdims=True)
    e2 = jnp.exp(m2 - m1)
    p1 = 1.0 / (1.0 + e2)
    p2 = e2 * p1
    route = jnp.where(lane == 0, i1.astype(F32),
                      jnp.where(lane == 1, i2.astype(F32),
                                jnp.where(lane == 2, p1, jnp.where(lane == 3, p2, 0.0))))
    route_ref[...] = route


def _merge(o, p, ypool, xs, mod_l, mod_row, o_gain, post_gain, pre2_gain, w_rec, w_pool, w_out,
           router=None):
    b, n, d = xs.shape
    moe = router is not None
    tm = _pick_tile(n, 512, SUBLANES)
    nt = n // tm
    if mod_row is None:
        mod_map = lambda bi, i: (bi, 0, 0)
    else:
        mod_map = lambda bi, i: (mod_row, 0, 0)
    g_blk = 4 * D_HGRN // D_HGRN
    ga_col = 5 * D_HGRN + D_POOL
    assert ga_col % d == 0
    ga_blk = ga_col // d
    const = lambda bi, i: (0, 0)
    in_specs = [
        pl.BlockSpec((1, tm, D_HGRN), lambda bi, i: (bi, i, 0)),
        pl.BlockSpec((1, tm, D_HGRN), lambda bi, i: (bi, i, g_blk)),
        pl.BlockSpec((1, tm, D_POOL), lambda bi, i: (bi, i, 0)),
        pl.BlockSpec((1, tm, d), lambda bi, i: (bi, i, ga_blk)),
        pl.BlockSpec((1, tm, d), lambda bi, i: (bi, i, ga_blk + 1)),
        pl.BlockSpec((1, tm, d), lambda bi, i: (bi, i, 0)),
        pl.BlockSpec((1, 6, d), mod_map),
        pl.BlockSpec((1, D_HGRN), const),
        pl.BlockSpec((1, d), const),
        pl.BlockSpec((1, d), const),
        pl.BlockSpec((D_HGRN, d), const),
        pl.BlockSpec((D_POOL, d), const),
        pl.BlockSpec((d, d), const),
    ]
    args = [o, p, ypool, p, p, xs, mod_l, o_gain.reshape(1, D_HGRN), post_gain.reshape(1, d),
            pre2_gain.reshape(1, d), w_rec, w_pool, w_out]
    out_specs = [pl.BlockSpec((1, tm, d), lambda bi, i: (bi, i, 0))]
    out_shape = [jax.ShapeDtypeStruct((b, n, d), F32)]
    n_experts = 0
    if moe:
        n_experts = router.shape[1]
        router_pad = jnp.zeros((d, LANES), F32).at[:, :n_experts].set(router)
        r_hi = router_pad.astype(BF16)
        r_lo = (router_pad - r_hi.astype(F32)).astype(BF16)
        in_specs.append(pl.BlockSpec((d, 2 * LANES), const))
        args.append(jnp.concatenate([r_hi, r_lo], axis=1))
        sub = d // LANES
        out_specs += [pl.BlockSpec((tm * sub, LANES), lambda bi, i: (bi * nt + i, 0)),
                      pl.BlockSpec((tm, LANES), lambda bi, i: (bi * nt + i, 0))]
        out_shape += [jax.ShapeDtypeStruct((b * n * sub, LANES), F32),
                      jax.ShapeDtypeStruct((b * n, LANES), F32)]
    else:
        out_specs.append(pl.BlockSpec((1, tm, d), lambda bi, i: (bi, i, 0)))
        out_shape.append(jax.ShapeDtypeStruct((b, n, d), BF16))
    return pl.pallas_call(
        functools.partial(_merge_kernel, moe=moe, n_experts=n_experts),
        grid=(b, nt),
        in_specs=in_specs,
        out_specs=out_specs,
        out_shape=out_shape,
        compiler_params=_cparams(("parallel", "parallel")),
        name="merge_moe" if moe else "merge",
    )(*args)


def _ffn_kernel(h_ref, wg_ref, wu_ref, wd_ref, x_ref, mod_ref, post_ref, o_ref, acc):
    j = pl.program_id(1)

    @pl.when(j == 0)
    def _():
        acc[...] = jnp.zeros_like(acc)

    h = h_ref[...]
    g = jnp.dot(h, wg_ref[...], preferred_element_type=F32)
    u = jnp.dot(h, wu_ref[...], preferred_element_type=F32)
    acc[...] += jnp.dot((_silu(g) * u).astype(BF16), wd_ref[...], preferred_element_type=F32)

    @pl.when(j == pl.num_programs(1) - 1)
    def _():
        m = mod_ref[0]
        o_ref[...] = x_ref[...] + m[5:6] * _rms(acc[...], post_ref[...])


def _ffn_dense(h2, xs, mod_l, mod_row, post_gain, wg, wu, wd):
    b, n, d = xs.shape
    f = wg.shape[1]
    r = b * n
    tm = _pick_tile(n, 512, SUBLANES)
    tf = _pick_tile(f, 1408, LANES)
    tiles_per_batch = n // tm
    if mod_row is None:
        mod_map = lambda i, j: (i // tiles_per_batch, 0, 0)
    else:
        mod_map = lambda i, j: (mod_row, 0, 0)
    out = pl.pallas_call(
        _ffn_kernel,
        grid=(r // tm, f // tf),
        in_specs=[
            pl.BlockSpec((tm, d), lambda i, j: (i, 0)),
            pl.BlockSpec((d, tf), lambda i, j: (0, j)),
            pl.BlockSpec((d, tf), lambda i, j: (0, j)),
            pl.BlockSpec((tf, d), lambda i, j: (j, 0)),
            pl.BlockSpec((tm, d), lambda i, j: (i, 0)),
            pl.BlockSpec((1, 6, d), mod_map),
            pl.BlockSpec((1, d), lambda i, j: (0, 0)),
        ],
        out_specs=pl.BlockSpec((tm, d), lambda i, j: (i, 0)),
        out_shape=jax.ShapeDtypeStruct((r, d), F32),
        scratch_shapes=[pltpu.VMEM((tm, d), F32)],
        compiler_params=_cparams(("parallel", "arbitrary")),
        name="ffn_dense",
    )(h2.reshape(r, d), wg, wu, wd, xs.reshape(r, d), mod_l, post_gain.reshape(1, d))
    return out.reshape(b, n, d)


def _moe_kernel(ie_ref, it_ref, ilo_ref, ihi_ref, ifirst_ref, ivalid_ref, iflush_ref,
                idx0_ref, idx_next_ref, dst_prev_ref, h_hbm, wg_ref, wu_ref, wd_ref, y_hbm,
                xg, ys, xb, acc, gsem, ssem, *, tm, sub, rps, nf):
    w = pl.program_id(0)
    j = pl.program_id(1)
    tile = it_ref[w]
    slot = tile % 2
    oslot = 1 - slot

    def gather_copy(ids_ref, dslot, r):
        src = h_hbm.at[pl.ds(pl.multiple_of(ids_ref[0, 0, r] * sub, sub), sub), :]
        dst = xg.at[dslot, pl.ds(pl.multiple_of(r * sub, sub), sub), :]
        return pltpu.make_async_copy(src, dst, gsem.at[dslot])

    def scatter_copy(r):
        src = ys.at[oslot, pl.ds(pl.multiple_of(r * sub, sub), sub), :]
        dst = y_hbm.at[pl.ds(pl.multiple_of(dst_prev_ref[0, 0, r] * sub, sub), sub), :]
        return pltpu.make_async_copy(src, dst, ssem.at[oslot])

    def wait_gather(dslot):
        pltpu.make_async_copy(h_hbm.at[pl.ds(0, tm * sub), :], xg.at[dslot], gsem.at[dslot]).wait()

    def wait_scatter():
        pltpu.make_async_copy(ys.at[oslot], y_hbm.at[pl.ds(0, tm * sub), :], ssem.at[oslot]).wait()

    @pl.when(j == 0)
    def _():
        @pl.when(w == 0)
        def _():
            def issue(r, carry):
                gather_copy(idx0_ref, 0, r).start()
                return carry
            lax.fori_loop(0, tm, issue, 0)
            ys[...] = jnp.zeros_like(ys)
            wait_gather(0)

        @pl.when(jnp.logical_and(w > 0, ivalid_ref[jnp.maximum(w - 1, 0)] > 0))
        def _():
            wait_gather((it_ref[jnp.maximum(w - 1, 0)] + 1) % 2)

        @pl.when(ifirst_ref[w] > 0)
        def _():
            for s in range(sub):
                xb[:, s * LANES:(s + 1) * LANES] = xg[slot, pl.ds(s, tm, stride=sub), :].astype(BF16)

        acc[...] = jnp.zeros_like(acc)

        @pl.when(iflush_ref[w] > 0)
        def _():
            def issue(r, carry):
                scatter_copy(r).start()
                return carry
            lax.fori_loop(0, tm, issue, 0)
            wait_scatter()

    @pl.when(ivalid_ref[w] > 0)
    def _():
        base = j * rps
        for k in range(rps):
            gather_copy(idx_next_ref, oslot, base + k).start()
            scatter_copy(base + k).start()
        h = xb[...]
        g = jnp.dot(h, wg_ref[0].astype(BF16), preferred_element_type=F32)
        u = jnp.dot(h, wu_ref[0].astype(BF16), preferred_element_type=F32)
        acc[...] += jnp.dot((_silu(g) * u).astype(BF16), wd_ref[0].astype(BF16),
                            preferred_element_type=F32)

        @pl.when(j == nf - 1)
        def _():
            for r in range(rps * nf, tm):
                gather_copy(idx_next_ref, oslot, r).start()
                scatter_copy(r).start()
            wait_scatter()
            whole = jnp.logical_and(ilo_ref[w] == 0, ihi_ref[w] == tm)

            @pl.when(whole)
            def _():
                for s in range(sub):
                    ys[slot, pl.ds(s, tm, stride=sub), :] = acc[:, s * LANES:(s + 1) * LANES]

            @pl.when(jnp.logical_not(whole))
            def _():
                rowi = lax.broadcasted_iota(jnp.int32, (tm, LANES), 0)
                mine = jnp.logical_and(rowi >= ilo_ref[w], rowi < ihi_ref[w])
                for s in range(sub):
                    old = ys[slot, pl.ds(s, tm, stride=sub), :]
                    ys[slot, pl.ds(s, tm, stride=sub), :] = jnp.where(
                        mine, acc[:, s * LANES:(s + 1) * LANES], old)


def _combine_kernel(y_ref, route_ref, x_ref, mod_ref, post_ref, o_ref, *, tm, sub):
    rt = route_ref[...]
    stride = TOP_K * sub
    y1 = jnp.concatenate([y_ref[pl.ds(s, tm, stride=stride), :] for s in range(sub)], axis=-1)
    y2 = jnp.concatenate([y_ref[pl.ds(sub + s, tm, stride=stride), :] for s in range(sub)], axis=-1)
    y = rt[:, 2:3] * y1 + rt[:, 3:4] * y2
    m = mod_ref[0]
    o_ref[...] = x_ref[...] + m[5:6] * _rms(y, post_ref[...])


def _ffn_moe(hrows, route, xs, mod_l, mod_row, post_gain, wg, wu, wd):
    b, n, d = xs.shape
    r = b * n
    n_pairs = TOP_K * r
    n_exp, _, f = wg.shape
    sub = d // LANES
    tm = _pick_tile(n_pairs, 1024, SUBLANES)
    tf = _pick_tile(f, 512, LANES)
    nf = f // tf
    rps = tm // nf
    n_tiles = n_pairs // tm
    n_items = n_tiles + n_exp

    e_flat = route[:, :TOP_K].astype(jnp.int32).reshape(-1)
    pair_ids = jnp.arange(n_pairs, dtype=jnp.int32)
    order = jnp.sort(e_flat * n_pairs + pair_ids) % n_pairs
    counts = jnp.sum((e_flat[:, None] == jnp.arange(n_exp, dtype=jnp.int32)[None, :]).astype(jnp.int32),
                     axis=0)
    cend = jnp.cumsum(counts)
    cstart = cend - counts
    first_tile = cstart // tm
    last_tile = jnp.maximum(cend - 1, 0) // tm
    n_e = jnp.where(counts > 0, last_tile - first_tile + 1, 0)
    item_end = jnp.cumsum(n_e)
    item_start = item_end - n_e
    total = item_end[-1]
    wi = jnp.arange(n_items, dtype=jnp.int32)
    item_e = jnp.minimum(jnp.sum((wi[:, None] >= item_end[None, :]).astype(jnp.int32), axis=1), n_exp - 1)
    item_valid = (wi < total).astype(jnp.int32)
    item_tile = jnp.where(item_valid > 0, first_tile[item_e] + wi - item_start[item_e], n_tiles)
    item_lo = jnp.clip(cstart[item_e] - item_tile * tm, 0, tm) * item_valid
    item_hi = jnp.clip(cend[item_e] - item_tile * tm, 0, tm) * item_valid
    prev_tile = jnp.concatenate([jnp.full((1,), -1, jnp.int32), item_tile[:-1]])
    item_first = jnp.logical_and(item_valid > 0, item_tile != prev_tile).astype(jnp.int32)
    item_flush = (wi == total).astype(jnp.int32)
    last_e = item_e[jnp.maximum(total - 1, 0)]
    item_e = jnp.where(item_valid > 0, item_e, last_e)

    src3 = (order // TOP_K).reshape(n_tiles, 1, tm)
    dst3 = order.reshape(n_tiles, 1, tm)

    def w_col(w, j, ie, it, ilo, ihi, ifi, iva, ifl):
        return (ie[w], 0, jnp.where(iva[w] > 0, j, nf - 1))

    def w_row(w, j, ie, it, ilo, ihi, ifi, iva, ifl):
        return (ie[w], jnp.where(iva[w] > 0, j, nf - 1), 0)

    smem = pltpu.MemorySpace.SMEM
    y2 = pl.pallas_call(
        functools.partial(_moe_kernel, tm=tm, sub=sub, rps=rps, nf=nf),
        grid_spec=pltpu.PrefetchScalarGridSpec(
            num_scalar_prefetch=7,
            grid=(n_items, nf),
            in_specs=[
                pl.BlockSpec((1, 1, tm), lambda w, j, *_: (0, 0, 0), memory_space=smem),
                pl.BlockSpec((1, 1, tm), lambda w, j, ie, it, *_: (jnp.minimum(it[w] + 1, n_tiles - 1), 0, 0),
                             memory_space=smem),
                pl.BlockSpec((1, 1, tm), lambda w, j, ie, it, *_: (jnp.clip(it[w] - 1, 0, n_tiles - 1), 0, 0),
                             memory_space=smem),
                pl.BlockSpec(memory_space=pl.ANY),
                pl.BlockSpec((1, d, tf), w_col),
                pl.BlockSpec((1, d, tf), w_col),
                pl.BlockSpec((1, tf, d), w_row),
            ],
            out_specs=pl.BlockSpec(memory_space=pl.ANY),
            scratch_shapes=[
                pltpu.VMEM((2, tm * sub, LANES), F32),
                pltpu.VMEM((2, tm * sub, LANES), F32),
                pltpu.VMEM((tm, d), BF16),
                pltpu.VMEM((tm, d), F32),
                pltpu.SemaphoreType.DMA((2,)),
                pltpu.SemaphoreType.DMA((2,)),
            ],
        ),
        out_shape=jax.ShapeDtypeStruct((n_pairs * sub, LANES), F32),
        compiler_params=_cparams(("arbitrary", "arbitrary")),
        name="moe_experts",
    )(item_e, item_tile, item_lo, item_hi, item_first, item_valid, item_flush,
      src3, src3, dst3, hrows, wg, wu, wd)

    tmc = _pick_tile(n, 512, SUBLANES)
    tiles_per_batch = n // tmc
    if mod_row is None:
        mod_map = lambda i: (i // tiles_per_batch, 0, 0)
    else:
        mod_map = lambda i: (mod_row, 0, 0)
    out = pl.pallas_call(
        functools.partial(_combine_kernel, tm=tmc, sub=sub),
        grid=(r // tmc,),
        in_specs=[
            pl.BlockSpec((tmc * TOP_K * sub, LANES), lambda i: (i, 0)),
            pl.BlockSpec((tmc, LANES), lambda i: (i, 0)),
            pl.BlockSpec((tmc, d), lambda i: (i, 0)),
            pl.BlockSpec((1, 6, d), mod_map),
            pl.BlockSpec((1, d), lambda i: (0, 0)),
        ],
        out_specs=pl.BlockSpec((tmc, d), lambda i: (i, 0)),
        out_shape=jax.ShapeDtypeStruct((r, d), F32),
        compiler_params=_cparams(("parallel",)),
        name="moe_combine",
    )(y2, route, xs.reshape(r, d), mod_l, post_gain.reshape(1, d))
    return out.reshape(b, n, d)


def kernel(x, c, ctx, c_ctx, ada_w, ada_b, mix_pre, mix_post, ffn_pre, ffn_post, w_in, hgrn_lb, hgrn_gain,
           pool_w, pool_scale, w_branch_rec, w_branch_pool, w_out, ffn_w_gate, ffn_w_up, ffn_w_down,
           moe_router, moe_w_gate, moe_w_up, moe_w_down):
    b, n, d = x.shape
    depth = ada_w.shape[0]
    assert b + 1 <= MOD_ROWS and d % LANES == 0
    ctx_row = b

    cond = jnp.zeros((MOD_ROWS, d), F32).at[:b].set(c).at[b].set(c_ctx)
    mod = _modulation(cond, ada_w, ada_b).reshape(depth, MOD_ROWS, 6, d)

    lb_all = jnp.cumsum(jax.nn.softmax(hgrn_lb.astype(F32), axis=0), axis=0)
    lb_all = lb_all - lb_all[0:1]

    bf = lambda a: a.astype(BF16)
    w_in_b, pool_w_b = bf(w_in), bf(pool_w)
    w_rec_b, w_pool_b, w_out_b = bf(w_branch_rec), bf(w_branch_pool), bf(w_out)
    ffn_g, ffn_u, ffn_d = bf(ffn_w_gate), bf(ffn_w_up), bf(ffn_w_down)
    moe_g, moe_u, moe_d = moe_w_gate, moe_w_up, moe_w_down

    u_blk = (5 * D_HGRN) // D_POOL
    s_zero = jnp.zeros((b, HEADS, HEAD_DIM, HEAD_DIM), F32)

    def mixer(xs, l, mod_row, s0f, s0b, on_grid, router):
        p = _in_proj(xs, mod[l], mod_row, mix_pre[l], w_in_b[l])
        o, s_f, s_b = _gla(p, lb_all[l], s0f, s0b, with_out=True, col0=0)
        yp = _pool(p, pool_w_b[l], pool_scale[l], on_grid=on_grid, col_block=u_blk)
        outs = _merge(o, p, yp, xs, mod[l], mod_row, hgrn_gain[l], mix_post[l], ffn_pre[l],
                      w_rec_b[l], w_pool_b[l], w_out_b[l], router=router)
        return outs, s_f, s_b

    def channel(outs, l, mod_row):
        j = l // 2
        if l % 2 == 0:
            xn, h2 = outs
            return _ffn_dense(h2, xn, mod[l], mod_row, ffn_post[l], ffn_g[j], ffn_u[j], ffn_d[j])
        xn, hrows, route = outs
        return _ffn_moe(hrows, route, xn, mod[l], mod_row, ffn_post[l], moe_g[j], moe_u[j], moe_d[j])

    cx = ctx
    for l in range(depth):
        last = l == depth - 1
        router = moe_router[l // 2] if l % 2 == 1 else None
        if last:
            pc = _in_proj(cx, mod[l], ctx_row, mix_pre[l], w_in_b[l][:, D_HGRN:4 * D_HGRN])
            s_f, s_b = _gla(pc, lb_all[l], s_zero, s_zero, with_out=False, col0=0)
            c_outs = None
        else:
            c_outs, s_f, s_b = mixer(cx, l, ctx_row, s_zero, s_zero, False, router)
        x_outs, _, _ = mixer(x, l, None, s_f, s_b, True, router)
        x = channel(x_outs, l, None)
        if not last:
            cx = channel(c_outs, l, ctx_row)
    return x
```

```python
import functools

import jax
import jax.numpy as jnp
from jax import lax
from jax.experimental import pallas as pl
from jax.experimental.pallas import tpu as pltpu

F32 = jnp.float32
BF16 = jnp.bfloat16

EPS = 1e-6
LANES = 128
SUBLANES = 8
HEADS = 4
HEAD_DIM = 128
D_HGRN = HEADS * HEAD_DIM
D_POOL = 512
POOL_WINDOWS = (2, 4, 8, 16)
POOL_GROUP = D_POOL // len(POOL_WINDOWS)
GRID_W = 64
CHUNK = 64
SUBCHUNK = CHUNK // 2
SAFE_DECAY = 85.0
DIAG = SUBLANES
GLA_HEADS_PER_STEP = 2
STATE_UNROLL = 8
ATT_UNROLL = 4
TOP_K = 2
MOD_ROWS = 40
VMEM_LIMIT = 56 * 1024 * 1024

def _cparams(sem):
    return pltpu.CompilerParams(dimension_semantics=sem, vmem_limit_bytes=VMEM_LIMIT)


def _pick_tile(n, cap, mult):
    if n <= cap:
        return n
    best = None
    for t in range(mult, cap + 1, mult):
        if n % t == 0:
            best = t
    assert best is not None, (n, cap, mult)
    return best


def _sigmoid(z):
    return 1.0 / (1.0 + jnp.exp(-z))


def _silu(z):
    return z * _sigmoid(z)


def _mod_kernel(c_ref, w_ref, b_ref, o_ref):
    a = _silu(c_ref[...])
    o_ref[0] = jnp.dot(a, w_ref[0], preferred_element_type=F32,
                       precision=lax.Precision.HIGHEST) + b_ref[0]


def _modulation(cond, ada_w, ada_b):
    depth, d, six_d = ada_w.shape
    tn = _pick_tile(six_d, 1024, LANES)
    return pl.pallas_call(
        _mod_kernel,
        grid=(depth, six_d // tn),
        in_specs=[
            pl.BlockSpec((MOD_ROWS, d), lambda l, j: (0, 0)),
            pl.BlockSpec((1, d, tn), lambda l, j: (l, 0, j)),
            pl.BlockSpec((1, 1, tn), lambda l, j: (l, 0, j)),
        ],
        out_specs=pl.BlockSpec((1, MOD_ROWS, tn), lambda l, j: (l, 0, j)),
        out_shape=jax.ShapeDtypeStruct((depth, MOD_ROWS, six_d), F32),
        compiler_params=_cparams(("parallel", "parallel")),
        name="modulation",
    )(cond, ada_w, ada_b.reshape(depth, 1, six_d))


def _rms(x, gain):
    ms = jnp.mean(x * x, axis=-1, keepdims=True)
    return x * lax.rsqrt(ms + EPS) * gain


def _inproj_kernel(x_ref, mod_ref, gain_ref, w_ref, o_ref, *, tn):
    m = mod_ref[0]
    h = _rms(x_ref[0], gain_ref[...]) * (1.0 + m[1:2]) + m[0:1]
    hb = h.astype(BF16)
    for j in range(w_ref.shape[1] // tn):
        o_ref[0, :, j * tn:(j + 1) * tn] = jnp.dot(
            hb, w_ref[:, j * tn:(j + 1) * tn], preferred_element_type=F32).astype(o_ref.dtype)


def _in_proj(xs, mod_l, mod_row, gain, w):
    b, n, d = xs.shape
    nc = w.shape[1]
    tm = _pick_tile(n, 512, SUBLANES)
    tn = _pick_tile(nc, 512, LANES)
    if mod_row is None:
        mod_map = lambda bi, i: (bi, 0, 0)
    else:
        mod_map = lambda bi, i: (mod_row, 0, 0)
    return pl.pallas_call(
        functools.partial(_inproj_kernel, tn=tn),
        grid=(b, n // tm),
        in_specs=[
            pl.BlockSpec((1, tm, d), lambda bi, i: (bi, i, 0)),
            pl.BlockSpec((1, 6, d), mod_map),
            pl.BlockSpec((1, d), lambda bi, i: (0, 0)),
            pl.BlockSpec((d, nc), lambda bi, i: (0, 0)),
        ],
        out_specs=pl.BlockSpec((1, tm, nc), lambda bi, i: (bi, i, 0)),
        out_shape=jax.ShapeDtypeStruct((b, n, nc), BF16),
        compiler_params=_cparams(("parallel", "parallel")),
        name="in_proj",
    )(xs, mod_l, gain.reshape(1, d), w)


def _neg_abs(a):
    bits = lax.bitcast_convert_type(a, jnp.uint32) | jnp.uint32(0x80000000)
    return lax.bitcast_convert_type(bits, F32)


def _dot_nt(a, b):
    return lax.dot_general(a, b, (((1,), (1,)), ((), ())), preferred_element_type=F32)


def _dot_tn(a, b):
    return lax.dot_general(a, b, (((0,), (0,)), ((), ())), preferred_element_type=F32)


def _gla_kernel(*refs, n, with_out, hpb):
    if with_out:
        (q_ref, ff_ref, fb_ref, i_ref, lb_ref, s0f_ref, s0b_ref,
         o_ref, sf_ref, sb_ref, qs, qf_f, qf_b, vs, cum_f, cum_b, k_f, k_b, kh_f, kh_b) = refs
    else:
        (ff_ref, fb_ref, i_ref, lb_ref, s0f_ref, s0b_ref,
         sf_ref, sb_ref, vs, cum_f, cum_b, k_f, k_b, kh_f, kh_b) = refs
    c = CHUNK
    nchunks = n // c
    nblk = c // DIAG

    hb = SUBCHUNK
    pos_c = lax.broadcasted_iota(jnp.int32, (c, hpb * HEAD_DIM), 0) % hb

    def prepare(ci, low):
        rows = pl.ds(pl.multiple_of(ci * c, c), c)
        if with_out:
            qs[rows, :] = _silu(q_ref[0, rows, :].astype(F32)) * (HEAD_DIM ** -0.5)
        vs[rows, :] = i_ref[0, rows, :].astype(F32)
        for rev in (False, True):
            z = (fb_ref if rev else ff_ref)[0, rows, :].astype(F32)
            lb = lb_ref[1:2, :] if rev else lb_ref[0:1, :]
            e = jnp.exp(-jnp.abs(z))
            log_sig = jnp.minimum(z, 0.0) - jnp.log(1.0 + e)
            a = jnp.log(lb)
            bt = jnp.log(1.0 - lb) + log_sig
            mx = jnp.maximum(a, bt)
            mn = jnp.minimum(a, bt)
            s = mx + jnp.log(1.0 + jnp.exp(mn - mx))
            (k_b if rev else k_f)[rows, :] = (1.0 - lb) * (jnp.where(z >= 0.0, e, 1.0) / (1.0 + e))
            sh = 1
            while sh < hb:
                if rev:
                    s = s + jnp.where(pos_c < hb - sh, pltpu.roll(s, c - sh, 0), 0.0)
                else:
                    s = s + jnp.where(pos_c >= sh, pltpu.roll(s, sh, 0), 0.0)
                sh *= 2
            (cum_b if rev else cum_f)[rows, :] = s
            low = jnp.minimum(low, s)
        return low

    lowest = jnp.min(lax.fori_loop(0, nchunks, prepare, jnp.zeros((c, hpb * HEAD_DIM), F32)))

    row = lax.broadcasted_iota(jnp.int32, (c, c), 0)
    col = lax.broadcasted_iota(jnp.int32, (c, c), 1)
    rowv = lax.broadcasted_iota(jnp.int32, (c, HEAD_DIM), 0)
    row3 = lax.broadcasted_iota(jnp.int32, (nblk, DIAG, HEAD_DIM), 1)

    def level_map(rev):
        lv = jnp.zeros((c, c), jnp.int32)
        m = c
        while m > 1:
            half = m // 2
            t_right = (row % m) >= half
            s_right = (col % m) >= half
            pair = jnp.logical_and(jnp.logical_not(t_right), s_right) if rev else \
                jnp.logical_and(t_right, jnp.logical_not(s_right))
            lv = jnp.where(jnp.logical_and((row // m) == (col // m), pair), m, lv)
            m = half
        return lv

    lv_f, lv_b = level_map(False), level_map(True)
    t_rows_of = {}
    m = c
    while m > 1:
        in_right = (rowv % m) >= m // 2
        t_rows_of[(False, m)] = in_right
        t_rows_of[(True, m)] = jnp.logical_not(in_right)
        m //= 2

    far_of = {False: rowv >= hb, True: rowv < hb}

    def one_chunk(rev, cidx, st, accumulate, hh):
        r0 = pl.multiple_of(cidx * c, c)
        hs = slice(hh * HEAD_DIM, (hh + 1) * HEAD_DIM)
        loc = (cum_b if rev else cum_f)[pl.ds(r0, c), hs]
        kc = (k_b if rev else k_f)[pl.ds(r0, c), hs]
        vc = vs[pl.ds(r0, c), hs]
        vcb = vc.astype(BF16)
        far = far_of[rev]
        first = loc[hb:hb + 1, :] if rev else loc[hb - 1:hb, :]
        end = loc[0:1, :] if rev else loc[c - 1:c, :]
        tot = first + end

        def put(o):
            if accumulate:
                o_ref[0, pl.ds(r0, c), hs] = o_ref[0, pl.ds(r0, c), hs] + o
            else:
                o_ref[0, pl.ds(r0, c), hs] = o

        cum = loc + jnp.where(far, first, 0.0)
        if with_out:
            qc = qs[pl.ds(r0, c), hs]
            o = _dot_nt((qc * jnp.exp(cum)).astype(BF16), st.astype(BF16))
            lv = lv_b if rev else lv_f
            c3 = cum.reshape(nblk, DIAG, HEAD_DIM)
            att = jnp.zeros((c, c), F32)
            m = c
            while m > 1:
                half = m // 2
                t_rows = t_rows_of[(rev, m)]
                if m > DIAG:
                    pieces = []
                    for blk in range(c // m):
                        rr = blk * m + (half if rev else half - 1)
                        pieces.append(jnp.broadcast_to(cum[rr:rr + 1, :], (m, HEAD_DIM)))
                    ref = pieces[0] if len(pieces) == 1 else jnp.concatenate(pieces, axis=0)
                elif m > 2:
                    ref3 = None
                    for grp in reversed(range(DIAG // m)):
                        rr = grp * m + (half if rev else half - 1)
                        piece = jnp.broadcast_to(c3[:, rr:rr + 1, :], (nblk, DIAG, HEAD_DIM))
                        ref3 = piece if ref3 is None else jnp.where(row3 < (grp + 1) * m, piece, ref3)
                    ref = ref3.reshape(c, HEAD_DIM)
                else:
                    ref = jnp.where(t_rows, pltpu.roll(cum, (c - 1) if rev else 1, 0), cum)
                w = jnp.where(t_rows, qc, kc) * jnp.exp(_neg_abs(cum - ref))
                wb = w.astype(BF16)
                att = jnp.where(lv == m, _dot_nt(wb, wb), att)
                m = half
            o = o + jnp.dot(att.astype(BF16), vcb, preferred_element_type=F32)
            put(o + jnp.sum(qc * kc, axis=-1, keepdims=True) * vc)

        khat = kc * jnp.exp(tot - cum)
        return jnp.exp(tot) * st + _dot_tn(vcb, khat.astype(BF16))

    def make_body(accumulate):
        def body(ci, carry):
            st_f = tuple(one_chunk(False, ci, carry[hh], accumulate, hh) for hh in range(hpb))
            st_b = tuple(one_chunk(True, nchunks - 1 - ci, carry[hpb + hh], accumulate, hh)
                         for hh in range(hpb))
            return st_f + st_b
        return body

    def initial_states():
        return tuple(s0f_ref[0, hh] for hh in range(hpb)) + tuple(s0b_ref[0, hh] for hh in range(hpb))

    def put_states(carry):
        for hh in range(hpb):
            sf_ref[0, hh] = carry[hh]
            sb_ref[0, hh] = carry[hpb + hh]

    def sweep_midpoint():
        carry = lax.fori_loop(0, nchunks // 2, make_body(False), initial_states())
        put_states(lax.fori_loop(nchunks // 2, nchunks, make_body(True), carry))

    ab = 2 * c
    rowa = lax.broadcasted_iota(jnp.int32, (ab, ab), 0)
    cola = lax.broadcasted_iota(jnp.int32, (ab, ab), 1)
    rowb = lax.broadcasted_iota(jnp.int32, (ab, HEAD_DIM), 0)

    same_chunk = (rowa // c) == (cola // c)
    keep_of = {False: jnp.logical_and(same_chunk, cola <= rowa),
               True: jnp.logical_and(same_chunk, cola >= rowa)}
    far_b = {False: (rowb % c) >= hb, True: (rowb % c) < hb}

    def chunk_rows(x, r):
        return jnp.concatenate(
            [jnp.broadcast_to(x[ch * c + r:ch * c + r + 1, :], (c, HEAD_DIM)) for ch in range(ab // c)],
            axis=0)

    def att_block(rev, r0, hh):
        hs = slice(hh * HEAD_DIM, (hh + 1) * HEAD_DIM)
        loc = (cum_b if rev else cum_f)[pl.ds(r0, ab), hs]
        far = far_b[rev]
        e_first = jnp.exp(chunk_rows(loc, hb if rev else hb - 1))
        e_end = jnp.exp(chunk_rows(loc, 0 if rev else c - 1))
        k_loc = (k_b if rev else k_f)[pl.ds(r0, ab), hs] * jnp.exp(-loc)
        (kh_b if rev else kh_f)[pl.ds(r0, ab), hs] = (
            k_loc * jnp.where(far, e_end, e_end * e_first)).astype(BF16)
        if not with_out:
            return None
        q_loc = qs[pl.ds(r0, ab), hs] * jnp.exp(loc)
        (qf_b if rev else qf_f)[pl.ds(r0, ab), hs] = (q_loc * jnp.where(far, e_first, 1.0)).astype(BF16)
        q2 = jnp.concatenate([jnp.where(far, 0.0, q_loc), jnp.where(far, q_loc, 0.0)], axis=1)
        k2 = jnp.concatenate([k_loc, jnp.where(far, k_loc, k_loc * e_first)], axis=1)
        att = jnp.where(keep_of[rev], _dot_nt(q2.astype(BF16), k2.astype(BF16)), 0.0)
        return jnp.dot(att.astype(BF16), vs[pl.ds(r0, ab), hs].astype(BF16), preferred_element_type=F32)

    att_unroll = min(ATT_UNROLL, n // ab)
    state_unroll = min(STATE_UNROLL, nchunks)

    def att_body(gi, carry):
        for u in range(att_unroll):
            r0 = pl.multiple_of((gi * att_unroll + u) * ab, ab)
            for hh in range(hpb):
                o_f = att_block(False, r0, hh)
                o_b = att_block(True, r0, hh)
                if with_out:
                    o_ref[0, pl.ds(r0, ab), hh * HEAD_DIM:(hh + 1) * HEAD_DIM] = o_f + o_b
        return carry

    def state_chunk(rev, cidx, st, hh):
        r0 = pl.multiple_of(cidx * c, c)
        hs = slice(hh * HEAD_DIM, (hh + 1) * HEAD_DIM)
        sums_ref = cum_b if rev else cum_f

        def row_at(off):
            grp = off // SUBLANES * SUBLANES
            blk = sums_ref[pl.ds(pl.multiple_of(r0 + grp, SUBLANES), SUBLANES), hs]
            return blk[off - grp:off - grp + 1, :]

        first = row_at(hb if rev else hb - 1)
        end = row_at(0 if rev else c - 1)
        if with_out:
            qf = (qf_b if rev else qf_f)[pl.ds(r0, c), hs]
            (k_b if rev else k_f)[pl.ds(r0, c), hs] = _dot_nt(qf, st.astype(BF16))
        kh = (kh_b if rev else kh_f)[pl.ds(r0, c), hs]
        return jnp.exp(first + end) * st + _dot_tn(vs[pl.ds(r0, c), hs].astype(BF16), kh)

    def state_body(gi, carry):
        for u in range(state_unroll):
            ci = gi * state_unroll + u
            st_f = tuple(state_chunk(False, ci, carry[hh], hh) for hh in range(hpb))
            st_b = tuple(state_chunk(True, nchunks - 1 - ci, carry[hpb + hh], hh) for hh in range(hpb))
            carry = st_f + st_b
        return carry

    def add_state_part(ci, carry):
        rows = pl.ds(pl.multiple_of(ci * c, c), c)
        o_ref[0, rows, :] = o_ref[0, rows, :] + k_f[rows, :] + k_b[rows, :]
        return carry

    def sweep_one_sided():
        lax.fori_loop(0, n // (ab * att_unroll), att_body, 0)
        put_states(lax.fori_loop(0, nchunks // state_unroll, state_body, initial_states()))
        if with_out:
            lax.fori_loop(0, nchunks, add_state_part, 0)

    in_range = lowest >= -SAFE_DECAY

    @pl.when(in_range)
    def _():
        sweep_one_sided()

    @pl.when(jnp.logical_not(in_range))
    def _():
        sweep_midpoint()


def _gla(p, lb, s0f, s0b, *, with_out, col0):
    b, n, _ = p.shape
    assert n % (2 * CHUNK) == 0
    assert (n // CHUNK) % min(STATE_UNROLL, n // CHUNK) == 0
    assert (n // (2 * CHUNK)) % min(ATT_UNROLL, n // (2 * CHUNK)) == 0
    hpb = GLA_HEADS_PER_STEP
    groups = HEADS // hpb
    wblk = hpb * HEAD_DIM
    assert col0 % hpb == 0

    def col_spec(k):
        return pl.BlockSpec((1, n, wblk), lambda bi, h, k=k: (bi, 0, col0 // hpb + k * groups + h))

    st_spec = pl.BlockSpec((1, hpb, HEAD_DIM, HEAD_DIM), lambda bi, h: (bi, h, 0, 0))
    n_cols = 4 if with_out else 3
    in_specs = [col_spec(k) for k in range(n_cols)] + [
        pl.BlockSpec((2, wblk), lambda bi, h: (0, h)), st_spec, st_spec]
    st_shape = jax.ShapeDtypeStruct((b, HEADS, HEAD_DIM, HEAD_DIM), F32)
    out_specs = [st_spec, st_spec]
    out_shape = [st_shape, st_shape]
    scratch = [pltpu.VMEM((n, wblk), F32) for _ in range(5)] + [pltpu.VMEM((n, wblk), BF16)] * 2
    if with_out:
        out_specs = [pl.BlockSpec((1, n, wblk), lambda bi, h: (bi, 0, h))] + out_specs
        out_shape = [jax.ShapeDtypeStruct((b, n, D_HGRN), F32)] + out_shape
        scratch = [pltpu.VMEM((n, wblk), F32)] + [pltpu.VMEM((n, wblk), BF16)] * 2 + scratch
    return pl.pallas_call(
        functools.partial(_gla_kernel, n=n, with_out=with_out, hpb=hpb),
        grid=(b, groups),
        in_specs=in_specs,
        out_specs=out_specs,
        out_shape=out_shape,
        scratch_shapes=scratch,
        compiler_params=_cparams(("parallel", "parallel")),
        name="gla" if with_out else "gla_state",
    )(*([p] * n_cols), lb, s0f, s0b)


def _pool_kernel(u_ref, pw_ref, ps_ref, o_ref, *, n, width):
    rows = n // width
    t = lax.broadcasted_iota(jnp.int32, (n, POOL_GROUP), 0)
    colp = t % width
    rowp = t // width

    def shift(a, delta, pos, length, stride):
        rolled = pltpu.roll(a, (delta * stride) % n, 0)
        ok = jnp.logical_and(pos - delta >= 0, pos - delta < length)
        return jnp.where(ok, rolled, 0.0)

    def box_sum(a, w, pos, length, stride):
        trail, lead = a, a
        h = 1
        while h < w // 2:
            trail = trail + shift(trail, h, pos, length, stride)
            lead = lead + shift(lead, -h, pos, length, stride)
            h *= 2
        return shift(trail, 1, pos, length, stride) + lead

    def count(pos, w, length):
        lo = jnp.clip(pos - w // 2, 0, length)
        hi = jnp.clip(pos - w // 2 + w, 0, length)
        return (hi - lo).astype(F32)

    for gi, w in enumerate(POOL_WINDOWS):
        ug = u_ref[0, :, gi * POOL_GROUP:(gi + 1) * POOL_GROUP].astype(F32)
        if rows > 1:
            s = box_sum(ug, w, rowp, rows, width) / count(rowp, w, rows)
            s = box_sum(s, w, colp, width, 1) / count(colp, w, width)
        else:
            s = box_sum(ug, w, colp, width, 1) / count(colp, w, width)
        y = jnp.dot((s - ug).astype(BF16), pw_ref[gi], preferred_element_type=F32)
        y = y * ps_ref[:, gi * POOL_GROUP:(gi + 1) * POOL_GROUP]
        o_ref[0, :, gi * POOL_GROUP:(gi + 1) * POOL_GROUP] = y.astype(o_ref.dtype)


def _pool(p, pool_w, pool_scale, *, on_grid, col_block):
    b, n, _ = p.shape
    width = GRID_W if on_grid else n
    assert n % width == 0 and width & (width - 1) == 0
    return pl.pallas_call(
        functools.partial(_pool_kernel, n=n, width=width),
        grid=(b,),
        in_specs=[
            pl.BlockSpec((1, n, D_POOL), lambda bi: (bi, 0, col_block)),
            pl.BlockSpec(pool_w.shape, lambda bi: (0, 0, 0)),
            pl.BlockSpec((1, D_POOL), lambda bi: (0, 0)),
        ],
        out_specs=pl.BlockSpec((1, n, D_POOL), lambda bi: (bi, 0, 0)),
        out_shape=jax.ShapeDtypeStruct((b, n, D_POOL), BF16),
        compiler_params=_cparams(("parallel",)),
        name="pool",
    )(p, pool_w, pool_scale.reshape(1, D_POOL))


def _merge_kernel(*refs, moe, n_experts):
    (o_ref, g_ref, yp_ref, ga_ref, gb_ref, x_ref, mod_ref, og_ref, post_ref, pre2_ref,
     wr_ref, wp_ref, wo_ref) = refs[:13]
    if moe:
        router_ref, xn_ref, hrow_ref, route_ref = refs[13:]
    else:
        xn_ref, h2_ref = refs[13:]
    tm = x_ref.shape[1]
    m = mod_ref[0]
    o = o_ref[0]
    parts = []
    for h in range(HEADS):
        oh = o[:, h * HEAD_DIM:(h + 1) * HEAD_DIM]
        parts.append(oh * lax.rsqrt(jnp.mean(oh * oh, axis=-1, keepdims=True) + EPS))
    on = jnp.concatenate(parts, axis=-1) * og_ref[...]
    y_rec = (on * _silu(g_ref[0].astype(F32))).astype(BF16)
    rec = jnp.dot(y_rec, wr_ref[...], preferred_element_type=F32)
    pool = jnp.dot(yp_ref[0], wp_ref[...], preferred_element_type=F32)
    merged = _sigmoid(ga_ref[0].astype(F32)) * rec + _sigmoid(gb_ref[0].astype(F32)) * pool
    y = jnp.dot(merged.astype(BF16), wo_ref[...], preferred_element_type=F32)
    xn = x_ref[0] + m[2:3] * _rms(y, post_ref[...])
    xn_ref[0] = xn
    h2 = _rms(xn, pre2_ref[...]) * (1.0 + m[4:5]) + m[3:4]
    if not moe:
        h2_ref[0] = h2.astype(h2_ref.dtype)
        return
    d = h2.shape[-1]
    for s in range(d // LANES):
        hrow_ref[pl.ds(s, tm, stride=d // LANES), :] = h2[:, s * LANES:(s + 1) * LANES]
    h_hi = h2.astype(BF16)
    h_lo = (h2 - h_hi.astype(F32)).astype(BF16)
    l2 = jnp.dot(h_hi, router_ref[...], preferred_element_type=F32)
    logits = (l2[:, :LANES] + l2[:, LANES:]
              + jnp.dot(h_lo, router_ref[:, :LANES], preferred_element_type=F32))
    lane = lax.broadcasted_iota(jnp.int32, logits.shape, 1)
    neg = jnp.float32(-jnp.inf)
    lg = jnp.where(lane < n_experts, logits, neg)
    m1 = jnp.max(lg, axis=-1, keepdims=True)
    i1 = jnp.min(jnp.where(lg == m1, lane, LANES), axis=-1, keepdims=True)
    lg2 = jnp.where(lane == i1, neg, lg)
    m2 = jnp.max(lg2, axis=-1, keepdims=True)
    i2 = jnp.min(jnp.where(lg2 == m2, lane, LANES), axis=-1, keepdims=True)
    e2 = jnp.exp(m2 - m1)
    p1 = 1.0 / (1.0 + e2)
    p2 = e2 * p1
    route = jnp.where(lane == 0, i1.astype(F32),
                      jnp.where(lane == 1, i2.astype(F32),
                                jnp.where(lane == 2, p1, jnp.where(lane == 3, p2, 0.0))))
    route_ref[...] = route


def _merge(o, p, ypool, xs, mod_l, mod_row, o_gain, post_gain, pre2_gain, w_rec, w_pool, w_out,
           router=None):
    b, n, d = xs.shape
    moe = router is not None
    tm = _pick_tile(n, 512, SUBLANES)
    nt = n // tm
    if mod_row is None:
        mod_map = lambda bi, i: (bi, 0, 0)
    else:
        mod_map = lambda bi, i: (mod_row, 0, 0)
    g_blk = 4 * D_HGRN // D_HGRN
    ga_col = 5 * D_HGRN + D_POOL
    assert ga_col % d == 0
    ga_blk = ga_col // d
    const = lambda bi, i: (0, 0)
    in_specs = [
        pl.BlockSpec((1, tm, D_HGRN), lambda bi, i: (bi, i, 0)),
        pl.BlockSpec((1, tm, D_HGRN), lambda bi, i: (bi, i, g_blk)),
        pl.BlockSpec((1, tm, D_POOL), lambda bi, i: (bi, i, 0)),
        pl.BlockSpec((1, tm, d), lambda bi, i: (bi, i, ga_blk)),
        pl.BlockSpec((1, tm, d), lambda bi, i: (bi, i, ga_blk + 1)),
        pl.BlockSpec((1, tm, d), lambda bi, i: (bi, i, 0)),
        pl.BlockSpec((1, 6, d), mod_map),
        pl.BlockSpec((1, D_HGRN), const),
        pl.BlockSpec((1, d), const),
        pl.BlockSpec((1, d), const),
        pl.BlockSpec((D_HGRN, d), const),
        pl.BlockSpec((D_POOL, d), const),
        pl.BlockSpec((d, d), const),
    ]
    args = [o, p, ypool, p, p, xs, mod_l, o_gain.reshape(1, D_HGRN), post_gain.reshape(1, d),
            pre2_gain.reshape(1, d), w_rec, w_pool, w_out]
    out_specs = [pl.BlockSpec((1, tm, d), lambda bi, i: (bi, i, 0))]
    out_shape = [jax.ShapeDtypeStruct((b, n, d), F32)]
    n_experts = 0
    if moe:
        n_experts = router.shape[1]
        router_pad = jnp.zeros((d, LANES), F32).at[:, :n_experts].set(router)
        r_hi = router_pad.astype(BF16)
        r_lo = (router_pad - r_hi.astype(F32)).astype(BF16)
        in_specs.append(pl.BlockSpec((d, 2 * LANES), const))
        args.append(jnp.concatenate([r_hi, r_lo], axis=1))
        sub = d // LANES
        out_specs += [pl.BlockSpec((tm * sub, LANES), lambda bi, i: (bi * nt + i, 0)),
                      pl.BlockSpec((tm, LANES), lambda bi, i: (bi * nt + i, 0))]
        out_shape += [jax.ShapeDtypeStruct((b * n * sub, LANES), F32),
                      jax.ShapeDtypeStruct((b * n, LANES), F32)]
    else:
        out_specs.append(pl.BlockSpec((1, tm, d), lambda bi, i: (bi, i, 0)))
        out_shape.append(jax.ShapeDtypeStruct((b, n, d), BF16))
    return pl.pallas_call(
        functools.partial(_merge_kernel, moe=moe, n_experts=n_experts),
        grid=(b, nt),
        in_specs=in_specs,
        out_specs=out_specs,
        out_shape=out_shape,
        compiler_params=_cparams(("parallel", "parallel")),
        name="merge_moe" if moe else "merge",
    )(*args)


def _ffn_kernel(h_ref, wg_ref, wu_ref, wd_ref, x_ref, mod_ref, post_ref, o_ref, acc):
    j = pl.program_id(1)

    @pl.when(j == 0)
    def _():
        acc[...] = jnp.zeros_like(acc)

    h = h_ref[...]
    g = jnp.dot(h, wg_ref[...], preferred_element_type=F32)
    u = jnp.dot(h, wu_ref[...], preferred_element_type=F32)
    acc[...] += jnp.dot((_silu(g) * u).astype(BF16), wd_ref[...], preferred_element_type=F32)

    @pl.when(j == pl.num_programs(1) - 1)
    def _():
        m = mod_ref[0]
        o_ref[...] = x_ref[...] + m[5:6] * _rms(acc[...], post_ref[...])


def _ffn_dense(h2, xs, mod_l, mod_row, post_gain, wg, wu, wd):
    b, n, d = xs.shape
    f = wg.shape[1]
    r = b * n
    tm = _pick_tile(n, 512, SUBLANES)
    tf = _pick_tile(f, 2816, LANES)
    once = pl.Buffered(1)
    tiles_per_batch = n // tm
    if mod_row is None:
        mod_map = lambda i, j: (i // tiles_per_batch, 0, 0)
    else:
        mod_map = lambda i, j: (mod_row, 0, 0)
    out = pl.pallas_call(
        _ffn_kernel,
        grid=(r // tm, f // tf),
        in_specs=[
            pl.BlockSpec((tm, d), lambda i, j: (i, 0)),
            pl.BlockSpec((d, tf), lambda i, j: (0, j), pipeline_mode=once),
            pl.BlockSpec((d, tf), lambda i, j: (0, j), pipeline_mode=once),
            pl.BlockSpec((tf, d), lambda i, j: (j, 0), pipeline_mode=once),
            pl.BlockSpec((tm, d), lambda i, j: (i, 0)),
            pl.BlockSpec((1, 6, d), mod_map),
            pl.BlockSpec((1, d), lambda i, j: (0, 0)),
        ],
        out_specs=pl.BlockSpec((tm, d), lambda i, j: (i, 0)),
        out_shape=jax.ShapeDtypeStruct((r, d), F32),
        scratch_shapes=[pltpu.VMEM((tm, d), F32)],
        compiler_params=_cparams(("parallel", "arbitrary")),
        name="ffn_dense",
    )(h2.reshape(r, d), wg, wu, wd, xs.reshape(r, d), mod_l, post_gain.reshape(1, d))
    return out.reshape(b, n, d)


def _moe_kernel(ie_ref, it_ref, ilo_ref, ihi_ref, ifirst_ref, ivalid_ref, iflush_ref,
                idx0_ref, idx_next_ref, dst_prev_ref, h_hbm, wg_ref, wu_ref, wd_ref, y_hbm,
                xg, ys, xb, acc, gsem, ssem, *, tm, sub, rps, nf):
    w = pl.program_id(0)
    j = pl.program_id(1)
    tile = it_ref[w]
    slot = tile % 2
    oslot = 1 - slot

    def gather_copy(ids_ref, dslot, r):
        src = h_hbm.at[pl.ds(pl.multiple_of(ids_ref[0, 0, r] * sub, sub), sub), :]
        dst = xg.at[dslot, pl.ds(pl.multiple_of(r * sub, sub), sub), :]
        return pltpu.make_async_copy(src, dst, gsem.at[dslot])

    def scatter_copy(r):
        src = ys.at[oslot, pl.ds(pl.multiple_of(r * sub, sub), sub), :]
        dst = y_hbm.at[pl.ds(pl.multiple_of(dst_prev_ref[0, 0, r] * sub, sub), sub), :]
        return pltpu.make_async_copy(src, dst, ssem.at[oslot])

    def wait_gather(dslot):
        pltpu.make_async_copy(h_hbm.at[pl.ds(0, tm * sub), :], xg.at[dslot], gsem.at[dslot]).wait()

    def wait_scatter():
        pltpu.make_async_copy(ys.at[oslot], y_hbm.at[pl.ds(0, tm * sub), :], ssem.at[oslot]).wait()

    @pl.when(j == 0)
    def _():
        @pl.when(w == 0)
        def _():
            def issue(r, carry):
                gather_copy(idx0_ref, 0, r).start()
                return carry
            lax.fori_loop(0, tm, issue, 0)
            ys[...] = jnp.zeros_like(ys)
            wait_gather(0)

        @pl.when(jnp.logical_and(w > 0, ivalid_ref[jnp.maximum(w - 1, 0)] > 0))
        def _():
            wait_gather((it_ref[jnp.maximum(w - 1, 0)] + 1) % 2)

        @pl.when(ifirst_ref[w] > 0)
        def _():
            for s in range(sub):
                xb[:, s * LANES:(s + 1) * LANES] = xg[slot, pl.ds(s, tm, stride=sub), :].astype(BF16)

        acc[...] = jnp.zeros_like(acc)

        @pl.when(iflush_ref[w] > 0)
        def _():
            def issue(r, carry):
                scatter_copy(r).start()
                return carry
            lax.fori_loop(0, tm, issue, 0)
            wait_scatter()

    @pl.when(ivalid_ref[w] > 0)
    def _():
        base = j * rps
        for k in range(rps):
            gather_copy(idx_next_ref, oslot, base + k).start()
            scatter_copy(base + k).start()
        h = xb[...]
        g = jnp.dot(h, wg_ref[0].astype(BF16), preferred_element_type=F32)
        u = jnp.dot(h, wu_ref[0].astype(BF16), preferred_element_type=F32)
        acc[...] += jnp.dot((_silu(g) * u).astype(BF16), wd_ref[0].astype(BF16),
                            preferred_element_type=F32)

        @pl.when(j == nf - 1)
        def _():
            for r in range(rps * nf, tm):
                gather_copy(idx_next_ref, oslot, r).start()
                scatter_copy(r).start()
            wait_scatter()
            whole = jnp.logical_and(ilo_ref[w] == 0, ihi_ref[w] == tm)

            @pl.when(whole)
            def _():
                for s in range(sub):
                    ys[slot, pl.ds(s, tm, stride=sub), :] = acc[:, s * LANES:(s + 1) * LANES]

            @pl.when(jnp.logical_not(whole))
            def _():
                rowi = lax.broadcasted_iota(jnp.int32, (tm, LANES), 0)
                mine = jnp.logical_and(rowi >= ilo_ref[w], rowi < ihi_ref[w])
                for s in range(sub):
                    old = ys[slot, pl.ds(s, tm, stride=sub), :]
                    ys[slot, pl.ds(s, tm, stride=sub), :] = jnp.where(
                        mine, acc[:, s * LANES:(s + 1) * LANES], old)


def _combine_kernel(y_ref, route_ref, x_ref, mod_ref, post_ref, o_ref, *, tm, sub):
    rt = route_ref[...]
    stride = TOP_K * sub
    y1 = jnp.concatenate([y_ref[pl.ds(s, tm, stride=stride), :] for s in range(sub)], axis=-1)
    y2 = jnp.concatenate([y_ref[pl.ds(sub + s, tm, stride=stride), :] for s in range(sub)], axis=-1)
    y = rt[:, 2:3] * y1 + rt[:, 3:4] * y2
    m = mod_ref[0]
    o_ref[...] = x_ref[...] + m[5:6] * _rms(y, post_ref[...])


def _ffn_moe(hrows, route, xs, mod_l, mod_row, post_gain, wg, wu, wd):
    b, n, d = xs.shape
    r = b * n
    n_pairs = TOP_K * r
    n_exp, _, f = wg.shape
    sub = d // LANES
    tm = _pick_tile(n_pairs, 1024, SUBLANES)
    tf = _pick_tile(f, 512, LANES)
    nf = f // tf
    rps = tm // nf
    n_tiles = n_pairs // tm
    n_items = n_tiles + n_exp

    e_flat = route[:, :TOP_K].astype(jnp.int32).reshape(-1)
    pair_ids = jnp.arange(n_pairs, dtype=jnp.int32)
    order = jnp.sort(e_flat * n_pairs + pair_ids) % n_pairs
    counts = jnp.sum((e_flat[:, None] == jnp.arange(n_exp, dtype=jnp.int32)[None, :]).astype(jnp.int32),
                     axis=0)
    cend = jnp.cumsum(counts)
    cstart = cend - counts
    first_tile = cstart // tm
    last_tile = jnp.maximum(cend - 1, 0) // tm
    n_e = jnp.where(counts > 0, last_tile - first_tile + 1, 0)
    item_end = jnp.cumsum(n_e)
    item_start = item_end - n_e
    total = item_end[-1]
    wi = jnp.arange(n_items, dtype=jnp.int32)
    item_e = jnp.minimum(jnp.sum((wi[:, None] >= item_end[None, :]).astype(jnp.int32), axis=1), n_exp - 1)
    item_valid = (wi < total).astype(jnp.int32)
    item_tile = jnp.where(item_valid > 0, first_tile[item_e] + wi - item_start[item_e], n_tiles)
    item_lo = jnp.clip(cstart[item_e] - item_tile * tm, 0, tm) * item_valid
    item_hi = jnp.clip(cend[item_e] - item_tile * tm, 0, tm) * item_valid
    prev_tile = jnp.concatenate([jnp.full((1,), -1, jnp.int32), item_tile[:-1]])
    item_first = jnp.logical_and(item_valid > 0, item_tile != prev_tile).astype(jnp.int32)
    item_flush = (wi == total).astype(jnp.int32)
    last_e = item_e[jnp.maximum(total - 1, 0)]
    item_e = jnp.where(item_valid > 0, item_e, last_e)

    src3 = (order // TOP_K).reshape(n_tiles, 1, tm)
    dst3 = order.reshape(n_tiles, 1, tm)

    def w_col(w, j, ie, it, ilo, ihi, ifi, iva, ifl):
        return (ie[w], 0, jnp.where(iva[w] > 0, j, nf - 1))

    def w_row(w, j, ie, it, ilo, ihi, ifi, iva, ifl):
        return (ie[w], jnp.where(iva[w] > 0, j, nf - 1), 0)

    smem = pltpu.MemorySpace.SMEM
    y2 = pl.pallas_call(
        functools.partial(_moe_kernel, tm=tm, sub=sub, rps=rps, nf=nf),
        grid_spec=pltpu.PrefetchScalarGridSpec(
            num_scalar_prefetch=7,
            grid=(n_items, nf),
            in_specs=[
                pl.BlockSpec((1, 1, tm), lambda w, j, *_: (0, 0, 0), memory_space=smem),
                pl.BlockSpec((1, 1, tm), lambda w, j, ie, it, *_: (jnp.minimum(it[w] + 1, n_tiles - 1), 0, 0),
                             memory_space=smem),
                pl.BlockSpec((1, 1, tm), lambda w, j, ie, it, *_: (jnp.clip(it[w] - 1, 0, n_tiles - 1), 0, 0),
                             memory_space=smem),
                pl.BlockSpec(memory_space=pl.ANY),
                pl.BlockSpec((1, d, tf), w_col),
                pl.BlockSpec((1, d, tf), w_col),
                pl.BlockSpec((1, tf, d), w_row),
            ],
            out_specs=pl.BlockSpec(memory_space=pl.ANY),
            scratch_shapes=[
                pltpu.VMEM((2, tm * sub, LANES), F32),
                pltpu.VMEM((2, tm * sub, LANES), F32),
                pltpu.VMEM((tm, d), BF16),
                pltpu.VMEM((tm, d), F32),
                pltpu.SemaphoreType.DMA((2,)),
                pltpu.SemaphoreType.DMA((2,)),
            ],
        ),
        out_shape=jax.ShapeDtypeStruct((n_pairs * sub, LANES), F32),
        compiler_params=_cparams(("arbitrary", "arbitrary")),
        name="moe_experts",
    )(item_e, item_tile, item_lo, item_hi, item_first, item_valid, item_flush,
      src3, src3, dst3, hrows, wg, wu, wd)

    tmc = _pick_tile(n, 512, SUBLANES)
    tiles_per_batch = n // tmc
    if mod_row is None:
        mod_map = lambda i: (i // tiles_per_batch, 0, 0)
    else:
        mod_map = lambda i: (mod_row, 0, 0)
    out = pl.pallas_call(
        functools.partial(_combine_kernel, tm=tmc, sub=sub),
        grid=(r // tmc,),
        in_specs=[
            pl.BlockSpec((tmc * TOP_K * sub, LANES), lambda i: (i, 0)),
            pl.BlockSpec((tmc, LANES), lambda i: (i, 0)),
            pl.BlockSpec((tmc, d), lambda i: (i, 0)),
            pl.BlockSpec((1, 6, d), mod_map),
            pl.BlockSpec((1, d), lambda i: (0, 0)),
        ],
        out_specs=pl.BlockSpec((tmc, d), lambda i: (i, 0)),
        out_shape=jax.ShapeDtypeStruct((r, d), F32),
        compiler_params=_cparams(("parallel",)),
        name="moe_combine",
    )(y2, route, xs.reshape(r, d), mod_l, post_gain.reshape(1, d))
    return out.reshape(b, n, d)


def kernel(x, c, ctx, c_ctx, ada_w, ada_b, mix_pre, mix_post, ffn_pre, ffn_post, w_in, hgrn_lb, hgrn_gain,
           pool_w, pool_scale, w_branch_rec, w_branch_pool, w_out, ffn_w_gate, ffn_w_up, ffn_w_down,
           moe_router, moe_w_gate, moe_w_up, moe_w_down):
    b, n, d = x.shape
    depth = ada_w.shape[0]
    assert b + 1 <= MOD_ROWS and d % LANES == 0
    ctx_row = b

    cond = jnp.zeros((MOD_ROWS, d), F32).at[:b].set(c).at[b].set(c_ctx)
    mod = _modulation(cond, ada_w, ada_b).reshape(depth, MOD_ROWS, 6, d)

    lb_all = jnp.cumsum(jax.nn.softmax(hgrn_lb.astype(F32), axis=0), axis=0)
    lb_all = lb_all - lb_all[0:1]

    bf = lambda a: a.astype(BF16)
    w_in_b, pool_w_b = bf(w_in), bf(pool_w)
    w_rec_b, w_pool_b, w_out_b = bf(w_branch_rec), bf(w_branch_pool), bf(w_out)
    ffn_g, ffn_u, ffn_d = bf(ffn_w_gate), bf(ffn_w_up), bf(ffn_w_down)
    moe_g, moe_u, moe_d = moe_w_gate, moe_w_up, moe_w_down

    u_blk = (5 * D_HGRN) // D_POOL
    s_zero = jnp.zeros((b, HEADS, HEAD_DIM, HEAD_DIM), F32)

    def mixer(xs, l, mod_row, s0f, s0b, on_grid, router):
        p = _in_proj(xs, mod[l], mod_row, mix_pre[l], w_in_b[l])
        o, s_f, s_b = _gla(p, lb_all[l], s0f, s0b, with_out=True, col0=0)
        yp = _pool(p, pool_w_b[l], pool_scale[l], on_grid=on_grid, col_block=u_blk)
        outs = _merge(o, p, yp, xs, mod[l], mod_row, hgrn_gain[l], mix_post[l], ffn_pre[l],
                      w_rec_b[l], w_pool_b[l], w_out_b[l], router=router)
        return outs, s_f, s_b

    def channel(outs, l, mod_row):
        j = l // 2
        if l % 2 == 0:
            xn, h2 = outs
            return _ffn_dense(h2, xn, mod[l], mod_row, ffn_post[l], ffn_g[j], ffn_u[j], ffn_d[j])
        xn, hrows, route = outs
        return _ffn_moe(hrows, route, xn, mod[l], mod_row, ffn_post[l], moe_g[j], moe_u[j], moe_d[j])

    cx = ctx
    for l in range(depth):
        last = l == depth - 1
        router = moe_router[l // 2] if l % 2 == 1 else None
        if last:
            pc = _in_proj(cx, mod[l], ctx_row, mix_pre[l], w_in_b[l][:, D_HGRN:4 * D_HGRN])
            s_f, s_b = _gla(pc, lb_all[l], s_zero, s_zero, with_out=False, col0=0)
            c_outs = None
        else:
            c_outs, s_f, s_b = mixer(cx, l, ctx_row, s_zero, s_zero, False, router)
        x_outs, _, _ = mixer(x, l, None, s_f, s_b, True, router)
        x = channel(x_outs, l, None)
        if not last:
            cx = channel(c_outs, l, ctx_row)
    return x
```
